```python
import math
import jax, jax.numpy as jnp
from jax import lax
import numpy as np


D_MODEL = 4096
BATCH = 2
SEQ = 8192
DEPTH = 4
DEC_BATCH = 8
DEC_SEQ = 16
PAST_LEN = 2048

CHUNK = 64
N_A = DEPTH // 2
N_B = DEPTH - N_A
MEM_TOKENS = 256
MEM_HEADS = 4
MEM_HEAD_DIM = D_MODEL // 16
MEM_WIDTH = MEM_HEADS * MEM_HEAD_DIM
MIX_WIDTH = D_MODEL - MEM_WIDTH
CONV_WIDTH = 31
CONV_STATE = CONV_WIDTH - 1
FOX_HEAD_DIM = 128
N_Q_HEADS = MIX_WIDTH // FOX_HEAD_DIM
N_KV_HEADS = N_Q_HEADS // 3
GROUP = N_Q_HEADS // N_KV_HEADS
KV_WIDTH = N_KV_HEADS * FOX_HEAD_DIM
D_FF = 4 * D_MODEL
Q_BLOCK = 128
EPS = 1e-6
NEG_INF = -1e30

kernel_name = 'yoco_conformer_fox_memory_stream_step'


def rmsnorm(x, g):
    xf = x.astype(jnp.float32)
    y = xf * lax.rsqrt(jnp.mean(xf * xf, axis=-1, keepdims=True) + EPS)
    return (y * g.astype(jnp.float32)).astype(x.dtype)


def layernorm(x, g, b):
    xf = x.astype(jnp.float32)
    mu = jnp.mean(xf, axis=-1, keepdims=True)
    xc = xf - mu
    var = jnp.mean(xc * xc, axis=-1, keepdims=True)
    return xc * lax.rsqrt(var + EPS) * g.astype(jnp.float32) + b.astype(jnp.float32)


def causal_depthwise_conv(xpad, w, b):
    c = w.shape[-1]
    y = lax.conv_general_dilated(xpad, w[:, None, :], window_strides=(1,), padding='VALID',
                                 dimension_numbers=('NWC', 'WIO', 'NWC'), feature_group_count=c)
    return y + b


def memory_kv(mem, g, w):
    bsz, m, _ = mem.shape
    mk, mv = jnp.split(rmsnorm(mem, g) @ w, 2, axis=-1)
    return (mk.reshape(bsz, m, MEM_HEADS, MEM_HEAD_DIM), mv.reshape(bsz, m, MEM_HEADS, MEM_HEAD_DIM))


def memory_attention(q, mk, mv):
    bsz, t, _ = q.shape
    qh = q.reshape(bsz, t, MEM_HEADS, MEM_HEAD_DIM)
    s = jnp.einsum('bthd,bmhd->bhtm', qh, mk).astype(jnp.float32) * (MEM_HEAD_DIM ** -0.5)
    p = jax.nn.softmax(s, axis=-1).astype(mv.dtype)
    return jnp.einsum('bhtm,bmhd->bthd', p, mv).reshape(bsz, t, MEM_WIDTH)


def forgetting_attention(q, k, v, cq, ck, q_pos, k_pos):
    bsz, t = q.shape[:2]
    s_len = k.shape[1]
    blk = Q_BLOCK if t % Q_BLOCK == 0 else t
    nb = t // blk
    qb = q.reshape(bsz, nb, blk, N_KV_HEADS, GROUP, FOX_HEAD_DIM).transpose(1, 0, 2, 3, 4, 5)
    cqb = cq.reshape(bsz, nb, blk, N_KV_HEADS, GROUP).transpose(1, 0, 2, 3, 4)
    pb = q_pos.reshape(nb, blk)
    ck_bias = ck.reshape(bsz, s_len, N_KV_HEADS, GROUP).transpose(0, 2, 3, 1)[:, :, :, None, :]
    scale = FOX_HEAD_DIM ** -0.5

    def one_block(args):
        qi, ci, pi = args
        sc = jnp.einsum('bqkgd,bskd->bkgqs', qi, k).astype(jnp.float32) * scale
        sc = sc + ci.transpose(0, 2, 3, 1)[..., None] - ck_bias
        mask = k_pos[None, :] <= pi[:, None]
        sc = jnp.where(mask, sc, NEG_INF)
        p = jax.nn.softmax(sc, axis=-1).astype(v.dtype)
        return jnp.einsum('bkgqs,bskd->bqkgd', p, v)

    o = lax.map(one_block, (qb, cqb, pb))
    return o.transpose(1, 0, 2, 3, 4, 5).reshape(bsz, t, N_Q_HEADS * FOX_HEAD_DIM)


def run_group(x, conv_hist, mem_k, mem_v, past_k, past_v, past_logf, w):
    bsz, t, _ = x.shape
    p_len = past_k.shape[1]
    q_pos = p_len + jnp.arange(t, dtype=jnp.int32)
    k_pos = jnp.arange(p_len + t, dtype=jnp.int32)
    conv_new = []
    k_all = v_all = c_all = None
    k_new = v_new = logf_new = None
    for layer in range(DEPTH):
        g = w['norm_g'][layer]
        h = rmsnorm(x, g[0])
        if layer < N_A:
            u = h @ w['w_in_a'][layer]
            a, gate, qm = jnp.split(u, [MIX_WIDTH, 2 * MIX_WIDTH], axis=-1)
            glu = a * jax.nn.sigmoid(gate)
            xpad = jnp.concatenate([conv_hist[layer].astype(glu.dtype), glu], axis=1)
            conv_new.append(xpad[:, -CONV_STATE:])
            c = causal_depthwise_conv(xpad, w['conv_w'][layer], w['conv_b'][layer])
            mix = jax.nn.silu(layernorm(c, w['conv_ln_g'][layer], w['conv_ln_b'][layer])).astype(x.dtype)
        else:
            u = h @ w['w_in_b'][layer - N_A]
            qf, qm = jnp.split(u, [MIX_WIDTH], axis=-1)
            qf = qf.reshape(bsz, t, N_Q_HEADS, FOX_HEAD_DIM)
            mix = forgetting_attention(qf, k_all, v_all, c_all[:, p_len:], c_all, q_pos, k_pos)
        mem_o = memory_attention(qm, mem_k[layer], mem_v[layer])
        o = jnp.concatenate([mix, mem_o], axis=-1) @ w['w_out'][layer]
        x = x + rmsnorm(o, g[1])
        f = jnp.square(jax.nn.relu(rmsnorm(x, g[2]) @ w['w_up'][layer])) @ w['w_down'][layer]
        x = x + rmsnorm(f, g[3])
        if layer == N_A - 1:
            s = rmsnorm(x, w['kv_norm_g'])
            kk, vv = jnp.split(s @ w['w_kv'], 2, axis=-1)
            k_new = kk.reshape(bsz, t, N_KV_HEADS, FOX_HEAD_DIM)
            v_new = vv.reshape(bsz, t, N_KV_HEADS, FOX_HEAD_DIM)
            logf_new = jax.nn.log_sigmoid((s @ w['w_f'] + w['b_f']).astype(jnp.float32))
            k_all = jnp.concatenate([past_k.astype(k_new.dtype), k_new], axis=1)
            v_all = jnp.concatenate([past_v.astype(v_new.dtype), v_new], axis=1)
            c_all = jnp.cumsum(jnp.concatenate([past_logf.astype(jnp.float32), logf_new], axis=1), axis=1)
    return x, jnp.stack(conv_new), k_new, v_new, logf_new


def setup_inputs(seed: int = 0) -> dict:
    key = jax.random.key(seed)
    ks = jax.random.split(key, 25)

    def nrm(k, shape, scale):
        return jax.random.normal(k, shape, jnp.float32) * scale

    def gain(k, shape):
        return 1.0 + 0.01 * jax.random.normal(k, shape, jnp.float32)

    return {
        'x_prompt': nrm(ks[0], (BATCH, SEQ, D_MODEL), 1.0),
        'x_sample': nrm(ks[1], (DEC_BATCH, DEC_SEQ, D_MODEL), 1.0),
        'cache_k': nrm(ks[2], (DEC_BATCH, PAST_LEN, N_KV_HEADS, FOX_HEAD_DIM), 1.0),
        'cache_v': nrm(ks[3], (DEC_BATCH, PAST_LEN, N_KV_HEADS, FOX_HEAD_DIM), 1.0),
        'cache_logf': jax.nn.log_sigmoid(2.5 + nrm(ks[4], (DEC_BATCH, PAST_LEN, N_Q_HEADS), 1.0)),
        'cache_mem_k': nrm(ks[5], (DEPTH, DEC_BATCH, MEM_TOKENS, MEM_HEADS, MEM_HEAD_DIM), 1.0),
        'cache_mem_v': nrm(ks[6], (DEPTH, DEC_BATCH, MEM_TOKENS, MEM_HEADS, MEM_HEAD_DIM), 1.0),
        'state_conv': nrm(ks[7], (N_A, DEC_BATCH, CONV_STATE, MIX_WIDTH), 0.5),
        'mem_prompt': nrm(ks[8], (BATCH, MEM_TOKENS, D_MODEL), 1.0),
        'norm_g': gain(ks[9], (DEPTH, 4, D_MODEL)),
        'mem_norm_g': gain(ks[10], (DEPTH, D_MODEL)),
        'w_mem_kv': nrm(ks[11], (DEPTH, D_MODEL, 2 * MEM_WIDTH), D_MODEL ** -0.5),
        'w_in_a': nrm(ks[12], (N_A, D_MODEL, 2 * MIX_WIDTH + MEM_WIDTH), D_MODEL ** -0.5),
        'conv_w': nrm(ks[13], (N_A, CONV_WIDTH, MIX_WIDTH), CONV_WIDTH ** -0.5),
        'conv_b': nrm(ks[14], (N_A, MIX_WIDTH), 0.02),
        'conv_ln_g': gain(ks[15], (N_A, MIX_WIDTH)),
        'conv_ln_b': nrm(ks[16], (N_A, MIX_WIDTH), 0.02),
        'w_in_b': nrm(ks[17], (N_B, D_MODEL, MIX_WIDTH + MEM_WIDTH), D_MODEL ** -0.5),
        'kv_norm_g': gain(ks[18], (D_MODEL,)),
        'w_kv': nrm(ks[19], (D_MODEL, 2 * KV_WIDTH), D_MODEL ** -0.5),
        'w_f': nrm(ks[20], (D_MODEL, N_Q_HEADS), D_MODEL ** -0.5),
        'b_f': jax.random.uniform(ks[21], (N_Q_HEADS,), jnp.float32, 1.0, 4.0),
        'w_out': nrm(ks[22], (DEPTH, D_MODEL, D_MODEL), D_MODEL ** -0.5),
        'w_up': nrm(ks[23], (DEPTH, D_MODEL, D_FF), D_MODEL ** -0.5),
        'w_down': nrm(ks[24], (DEPTH, D_FF, D_MODEL), D_FF ** -0.5),
    }


def reference(x_prompt, x_sample, cache_k, cache_v, cache_logf, cache_mem_k, cache_mem_v, state_conv,
              mem_prompt, norm_g, mem_norm_g, w_mem_kv, w_in_a, conv_w, conv_b, conv_ln_g, conv_ln_b,
              w_in_b, kv_norm_g, w_kv, w_f, b_f, w_out, w_up, w_down):
    w = {'norm_g': norm_g, 'w_in_a': w_in_a, 'conv_w': conv_w, 'conv_b': conv_b,
         'conv_ln_g': conv_ln_g, 'conv_ln_b': conv_ln_b, 'w_in_b': w_in_b, 'kv_norm_g': kv_norm_g,
         'w_kv': w_kv, 'w_f': w_f, 'b_f': b_f, 'w_out': w_out, 'w_up': w_up, 'w_down': w_down}

    mem_kv = [memory_kv(mem_prompt, mem_norm_g[l], w_mem_kv[l]) for l in range(DEPTH)]
    mem_k_prompt = jnp.stack([m[0] for m in mem_kv])
    mem_v_prompt = jnp.stack([m[1] for m in mem_kv])
    bsz = x_prompt.shape[0]
    zero_conv = jnp.zeros((N_A, bsz, CONV_STATE, MIX_WIDTH), x_prompt.dtype)
    empty_kv = jnp.zeros((bsz, 0, N_KV_HEADS, FOX_HEAD_DIM), x_prompt.dtype)
    empty_f = jnp.zeros((bsz, 0, N_Q_HEADS), jnp.float32)
    y_prompt, conv_prompt, k_prompt, v_prompt, logf_prompt = run_group(
        x_prompt, zero_conv, mem_k_prompt, mem_v_prompt, empty_kv, empty_kv, empty_f, w)

    y_sample, conv_sample, k_sample, v_sample, logf_sample = run_group(
        x_sample, state_conv, cache_mem_k, cache_mem_v, cache_k, cache_v, cache_logf, w)

    return (y_prompt, y_sample, k_prompt, v_prompt, logf_prompt, mem_k_prompt, mem_v_prompt, conv_prompt,
            k_sample, v_sample, logf_sample, conv_sample)
```

```python
import functools

import jax
import jax.numpy as jnp
from jax import lax
from jax.experimental import pallas as pl
from jax.experimental.pallas import tpu as pltpu

EPS = 1e-6
NEG_INF = -1e30
CONV_WIDTH = 31
CONV_STATE = CONV_WIDTH - 1
CONV_HALO = 32
FOX_HEAD_DIM = 128
GROUP = 3
MEM_HEADS = 4
LANES = 128
VMEM_LIMIT_BYTES = 56 * 1024 * 1024

BF16 = jnp.bfloat16
F32 = jnp.float32


def _params(*semantics):
    return pltpu.CompilerParams(dimension_semantics=semantics, vmem_limit_bytes=VMEM_LIMIT_BYTES)


def _tile(n, target, mult=8):
    if n <= target:
        return n
    for t in range(target, 0, -1):
        if n % t == 0 and t % mult == 0:
            return t
    return n


def _rms(xf, g):
    return xf * lax.rsqrt(jnp.mean(xf * xf, axis=-1, keepdims=True) + EPS) * g


def _norm_cast_kernel(x_ref, g_ref, o_ref):
    o_ref[...] = _rms(x_ref[...], g_ref[...]).astype(o_ref.dtype)


def norm_cast(x, g):
    m, d = x.shape
    nl = g.shape[0]
    tm = _tile(m, 256)
    return pl.pallas_call(
        _norm_cast_kernel,
        grid=(nl, m // tm),
        in_specs=[pl.BlockSpec((tm, d), lambda l, i: (i, 0)),
                  pl.BlockSpec((None, 1, d), lambda l, i: (l, 0, 0))],
        out_specs=pl.BlockSpec((None, tm, d), lambda l, i: (l, i, 0)),
        out_shape=jax.ShapeDtypeStruct((nl, m, d), BF16),
        compiler_params=_params("parallel", "parallel"),
        name="norm_cast",
    )(x, g)


def _residual_norm_kernel(x_ref, o_ref, gp_ref, *rest, n_next):
    g_refs = rest[:n_next]
    xo_ref = rest[n_next]
    h_refs = rest[n_next + 1:]
    xn = x_ref[...] + _rms(o_ref[...], gp_ref[...])
    xo_ref[...] = xn
    if n_next:
        inv = lax.rsqrt(jnp.mean(xn * xn, axis=-1, keepdims=True) + EPS)
        for g_ref, h_ref in zip(g_refs, h_refs):
            h_ref[...] = (xn * inv * g_ref[...]).astype(h_ref.dtype)


def residual_norm(x, o, g_post, g_next):
    m, d = x.shape
    tm = _tile(m, 256)
    n_next = len(g_next)
    row = pl.BlockSpec((tm, d), lambda i: (i, 0))
    gain = pl.BlockSpec((1, d), lambda i: (0, 0))
    outs = pl.pallas_call(
        functools.partial(_residual_norm_kernel, n_next=n_next),
        grid=(m // tm,),
        in_specs=[row, row, gain] + [gain] * n_next,
        out_specs=[row] + [row] * n_next,
        out_shape=[jax.ShapeDtypeStruct((m, d), F32)] + [jax.ShapeDtypeStruct((m, d), BF16)] * n_next,
        compiler_params=_params("parallel"),
        name="residual_norm",
    )(x, o, g_post.reshape(1, d), *[g.reshape(1, d) for g in g_next])
    return outs[0], list(outs[1:])


def _mm_kernel(x_ref, w_ref, *rest, nk, act, n_out):
    o_refs = rest[:n_out]
    acc_ref = rest[n_out] if nk > 1 else None

    def finish(acc):
        if act == "relu2":
            r = jnp.maximum(acc, 0.0)
            acc = r * r
        for o_ref in o_refs:
            o_ref[...] = acc.astype(o_ref.dtype)

    part = jnp.dot(x_ref[...], w_ref[...], preferred_element_type=F32)
    if nk == 1:
        finish(part)
        return
    k = pl.program_id(3)

    @pl.when(k == 0)
    def _():
        acc_ref[...] = part

    @pl.when(k > 0)
    def _():
        acc_ref[...] += part

    @pl.when(k == nk - 1)
    def _():
        finish(acc_ref[...])


def matmul(x, w, *, layer0=0, col0=0, ncols=None, out_dtypes=(F32,), act=None, tm=1024, tn=1024, tk=4096):
    nl, m, kdim = x.shape
    ncols = w.shape[2] - col0 if ncols is None else ncols
    tm, tn, tk = _tile(m, tm), _tile(ncols, tn, LANES), _tile(kdim, tk, LANES)
    nk = kdim // tk
    jb = col0 // tn
    assert col0 % tn == 0 and ncols % tn == 0
    n_out = len(out_dtypes)
    outs = pl.pallas_call(
        functools.partial(_mm_kernel, nk=nk, act=act, n_out=n_out),
        grid=(nl, m // tm, ncols // tn, nk),
        in_specs=[pl.BlockSpec((None, tm, tk), lambda l, i, j, k: (l, i, k)),
                  pl.BlockSpec((None, tk, tn), lambda l, i, j, k: (l + layer0, k, j + jb))],
        out_specs=[pl.BlockSpec((None, tm, tn), lambda l, i, j, k: (l, i, j))] * n_out,
        out_shape=[jax.ShapeDtypeStruct((nl, m, ncols), dt) for dt in out_dtypes],
        scratch_shapes=[pltpu.VMEM((tm, tn), F32)] if nk > 1 else [],
        compiler_params=_params("parallel", "parallel", "parallel", "arbitrary"),
        name="matmul_" + (act or "plain"),
    )(x, w)
    return list(outs)


def _glu_kernel(x_ref, wa_ref, wg_ref, o_ref):
    x = x_ref[...]
    a = jnp.dot(x, wa_ref[...], preferred_element_type=F32)
    gate = jnp.dot(x, wg_ref[...], preferred_element_type=F32)
    o_ref[...] = a * (1.0 / (1.0 + jnp.exp(-gate)))


def matmul_glu(x, w, layer, width, *, tm=1024, tn=512):
    m, kdim = x.shape
    tm, tn = _tile(m, tm), _tile(width, tn, LANES)
    nj = width // tn
    return pl.pallas_call(
        _glu_kernel,
        grid=(m // tm, nj),
        in_specs=[pl.BlockSpec((tm, kdim), lambda i, j: (i, 0)),
                  pl.BlockSpec((None, kdim, tn), lambda i, j: (layer, 0, j)),
                  pl.BlockSpec((None, kdim, tn), lambda i, j: (layer, 0, j + nj))],
        out_specs=pl.BlockSpec((tm, tn), lambda i, j: (i, j)),
        out_shape=jax.ShapeDtypeStruct((m, width), F32),
        compiler_params=_params("parallel", "parallel"),
        name="matmul_glu",
    )(x, w, w)


def _mm2_kernel(x1_ref, x2_ref, w1_ref, w2_ref, o_ref):
    o_ref[...] = (jnp.dot(x1_ref[...], w1_ref[...], preferred_element_type=F32)
                  + jnp.dot(x2_ref[...], w2_ref[...], preferred_element_type=F32))


def matmul_concat(x1, x2, w, layer, *, tm=1024, tn=1024):
    m, k1 = x1.shape
    k2 = x2.shape[1]
    n = w.shape[2]
    assert k1 % k2 == 0
    tm, tn = _tile(m, tm), _tile(n, tn, LANES)
    return pl.pallas_call(
        _mm2_kernel,
        grid=(m // tm, n // tn),
        in_specs=[pl.BlockSpec((tm, k1), lambda i, j: (i, 0)),
                  pl.BlockSpec((tm, k2), lambda i, j: (i, 0)),
                  pl.BlockSpec((None, k1, tn), lambda i, j: (layer, 0, j)),
                  pl.BlockSpec((None, k2, tn), lambda i, j: (layer, k1 // k2, j))],
        out_specs=pl.BlockSpec((tm, tn), lambda i, j: (i, j)),
        out_shape=jax.ShapeDtypeStruct((m, n), F32),
        compiler_params=_params("parallel", "parallel"),
        name="matmul_concat",
    )(x1, x2, w, w)


def _logf_kernel(x_ref, w_ref, b_ref, o_ref):
    z = jnp.dot(x_ref[...], w_ref[...], preferred_element_type=F32) + b_ref[...]
    o_ref[...] = jnp.minimum(z, 0.0) - jnp.log(1.0 + jnp.exp(-jnp.abs(z)))


def log_forget(s, w_f, b_f):
    m, d = s.shape
    n = w_f.shape[1]
    tm = _tile(m, 1024)
    return pl.pallas_call(
        _logf_kernel,
        grid=(m // tm,),
        in_specs=[pl.BlockSpec((tm, d), lambda i: (i, 0)),
                  pl.BlockSpec((d, n), lambda i: (0, 0)),
                  pl.BlockSpec((1, n), lambda i: (0, 0))],
        out_specs=pl.BlockSpec((tm, n), lambda i: (i, 0)),
        out_shape=jax.ShapeDtypeStruct((m, n), F32),
        compiler_params=_params("parallel"),
        name="log_forget",
    )(s, w_f, b_f)


def _cumsum_kernel(x_ref, o_ref, carry_ref, *, ts):
    @pl.when(pl.program_id(1) == 0)
    def _():
        carry_ref[...] = jnp.zeros_like(carry_ref)

    row = lax.broadcasted_iota(jnp.int32, (ts, ts), 0)
    col = lax.broadcasted_iota(jnp.int32, (ts, ts), 1)
    tri = (col <= row).astype(BF16)
    x = x_ref[...]
    x1 = x.astype(BF16)
    r1 = x - x1.astype(F32)
    x2 = r1.astype(BF16)
    x3 = (r1 - x2.astype(F32)).astype(BF16)
    y = (jnp.dot(tri, x1, preferred_element_type=F32)
         + jnp.dot(tri, x2, preferred_element_type=F32)
         + jnp.dot(tri, x3, preferred_element_type=F32)) + carry_ref[...]
    o_ref[...] = y
    carry_ref[...] = y[ts - 1:ts, :]


def cumsum_time(x):
    b, s, n = x.shape
    s_pad = -(-s // LANES) * LANES
    if s_pad != s:
        x = jnp.pad(x, ((0, 0), (0, s_pad - s), (0, 0)))
    ts = _tile(s_pad, 1024, LANES)
    y = pl.pallas_call(
        functools.partial(_cumsum_kernel, ts=ts),
        grid=(b, s_pad // ts),
        in_specs=[pl.BlockSpec((None, ts, n), lambda bi, i: (bi, i, 0))],
        out_specs=pl.BlockSpec((None, ts, n), lambda bi, i: (bi, i, 0)),
        out_shape=jax.ShapeDtypeStruct((b, s_pad, n), F32),
        scratch_shapes=[pltpu.VMEM((1, n), F32)],
        compiler_params=_params("parallel", "arbitrary"),
        name="cumsum_time",
    )(x)
    return y[:, :s]


def _conv_kernel(*refs, tt, has_prev):
    if has_prev:
        hist_ref, prev_ref, cur_ref, w_ref, b_ref, g_ref, lb_ref, o_ref, xs_ref = refs
    else:
        hist_ref, cur_ref, w_ref, b_ref, g_ref, lb_ref, o_ref, xs_ref = refs
    i = pl.program_id(1)

    @pl.when(i == 0)
    def _():
        xs_ref[0:CONV_HALO, :] = hist_ref[...]

    if has_prev:
        @pl.when(i > 0)
        def _():
            xs_ref[0:CONV_HALO, :] = prev_ref[...]

    xs_ref[CONV_HALO:CONV_HALO + tt, :] = cur_ref[...]
    first = CONV_HALO - CONV_STATE
    acc = w_ref[0:1, :] * xs_ref[pl.ds(first, tt), :]
    for k in range(1, CONV_WIDTH):
        acc = acc + w_ref[k:k + 1, :] * xs_ref[pl.ds(first + k, tt), :]
    c = acc + b_ref[...]
    mu = jnp.mean(c, axis=-1, keepdims=True)
    xc = c - mu
    var = jnp.mean(xc * xc, axis=-1, keepdims=True)
    y = xc * lax.rsqrt(var + EPS) * g_ref[...] + lb_ref[...]
    o_ref[...] = (y * (1.0 / (1.0 + jnp.exp(-y)))).astype(o_ref.dtype)


def conv_module(glu, hist, w, b, ln_g, ln_b, *, tt=64):
    bsz, t, c = glu.shape
    tt = _tile(t, tt, CONV_HALO) if t > tt else t
    nt = t // tt
    has_prev = nt > 1
    r = tt // CONV_HALO
    vec = pl.BlockSpec((1, c), lambda bi, i: (0, 0))
    in_specs = [pl.BlockSpec((None, CONV_HALO, c), lambda bi, i: (bi, 0, 0))]
    args = [hist]
    if has_prev:
        in_specs.append(pl.BlockSpec((None, CONV_HALO, c), lambda bi, i: (bi, jnp.maximum(i * r - 1, 0), 0)))
        args.append(glu)
    in_specs += [pl.BlockSpec((None, tt, c), lambda bi, i: (bi, i, 0)),
                 pl.BlockSpec((CONV_WIDTH, c), lambda bi, i: (0, 0)), vec, vec, vec]
    args += [glu, w, b.reshape(1, c), ln_g.reshape(1, c), ln_b.reshape(1, c)]
    return pl.pallas_call(
        functools.partial(_conv_kernel, tt=tt, has_prev=has_prev),
        grid=(bsz, nt),
        in_specs=in_specs,
        out_specs=pl.BlockSpec((None, tt, c), lambda bi, i: (bi, i, 0)),
        out_shape=jax.ShapeDtypeStruct((bsz, t, c), BF16),
        scratch_shapes=[pltpu.VMEM((CONV_HALO + tt, c), F32)],
        compiler_params=_params("parallel", "arbitrary"),
        name="conv_module",
    )(*args)


def _mem_attn_kernel(q_ref, k_ref, v_ref, o_ref, *, hd):
    scale = hd ** -0.5
    for h in range(MEM_HEADS):
        sl = slice(h * hd, (h + 1) * hd)
        s = lax.dot_general(q_ref[:, sl], k_ref[:, sl], (((1,), (1,)), ((), ())),
                            preferred_element_type=F32) * scale
        m = jnp.max(s, axis=-1, keepdims=True)
        p = jnp.exp(s - m)
        p = p / jnp.sum(p, axis=-1, keepdims=True)
        o_ref[:, sl] = jnp.dot(p.astype(BF16), v_ref[:, sl], preferred_element_type=F32).astype(o_ref.dtype)


def memory_attention(q_arr, q_col_block, mk, mv, *, tt=512):
    bsz, t, _ = q_arr.shape
    mtok, width = mk.shape[1:]
    tt = _tile(t, tt)
    return pl.pallas_call(
        functools.partial(_mem_attn_kernel, hd=width // MEM_HEADS),
        grid=(bsz, t // tt),
        in_specs=[pl.BlockSpec((None, tt, width), lambda bi, i: (bi, i, q_col_block)),
                  pl.BlockSpec((None, mtok, width), lambda bi, i: (bi, 0, 0)),
                  pl.BlockSpec((None, mtok, width), lambda bi, i: (bi, 0, 0))],
        out_specs=pl.BlockSpec((None, tt, width), lambda bi, i: (bi, i, 0)),
        out_shape=jax.ShapeDtypeStruct((bsz, t, width), BF16),
        compiler_params=_params("parallel", "parallel"),
        name="memory_attention",
    )(q_arr, mk, mv)


def _fox_kernel(q_ref, k_ref, v_ref, cq_ref, ck_ref, o_ref, m_ref, l_ref, acc_ref, *, tq, tk, p_len):
    qi = pl.program_id(2)
    ki = pl.program_id(3)
    scale = FOX_HEAD_DIM ** -0.5

    @pl.when(ki == 0)
    def _():
        m_ref[...] = jnp.full_like(m_ref, NEG_INF)
        l_ref[...] = jnp.zeros_like(l_ref)
        acc_ref[...] = jnp.zeros_like(acc_ref)

    q_first = p_len + qi * tq
    k_first = ki * tk

    @pl.when(k_first <= q_first + tq - 1)
    def _():
        k = k_ref[...]
        v = v_ref[...]
        q_pos = q_first + lax.broadcasted_iota(jnp.int32, (tq, tk), 0)
        k_pos = k_first + lax.broadcasted_iota(jnp.int32, (tq, tk), 1)
        visible = k_pos <= q_pos
        for g in range(GROUP):
            sl = slice(g * FOX_HEAD_DIM, (g + 1) * FOX_HEAD_DIM)
            s = lax.dot_general(q_ref[:, sl], k, (((1,), (1,)), ((), ())), preferred_element_type=F32) * scale
            s = s + cq_ref[:, g:g + 1] - ck_ref[g:g + 1, :]
            s = jnp.where(visible, s, NEG_INF)
            m_prev = m_ref[g]
            m_new = jnp.maximum(m_prev, jnp.max(s, axis=-1, keepdims=True))
            alpha = jnp.exp(m_prev - m_new)
            p = jnp.exp(s - m_new)
            l_ref[g] = alpha * l_ref[g] + jnp.sum(p, axis=-1, keepdims=True)
            acc_ref[g] = alpha * acc_ref[g] + jnp.dot(p.astype(BF16), v, preferred_element_type=F32)
            m_ref[g] = m_new

    @pl.when(ki == pl.num_programs(3) - 1)
    def _():
        for g in range(GROUP):
            sl = slice(g * FOX_HEAD_DIM, (g + 1) * FOX_HEAD_DIM)
            o_ref[:, sl] = (acc_ref[g] / l_ref[g]).astype(o_ref.dtype)


def forgetting_attention(q_arr, k_arr, k_col0, v_arr, v_col0, cq, ck, p_len, *, tq=512, tk=1024):
    bsz, t, _ = q_arr.shape
    s_len = k_arr.shape[1]
    n_kv = cq.shape[1]
    tq, tk = _tile(t, tq), _tile(s_len, tk, LANES)
    gw = GROUP * FOX_HEAD_DIM

    def kv_block(qi, ki):
        return jnp.minimum(ki, (p_len + (qi + 1) * tq - 1) // tk)

    return pl.pallas_call(
        functools.partial(_fox_kernel, tq=tq, tk=tk, p_len=p_len),
        grid=(bsz, n_kv, t // tq, s_len // tk),
        in_specs=[pl.BlockSpec((None, tq, gw), lambda b, h, qi, ki: (b, qi, h)),
                  pl.BlockSpec((None, tk, FOX_HEAD_DIM), lambda b, h, qi, ki: (b, kv_block(qi, ki), k_col0 + h)),
                  pl.BlockSpec((None, tk, FOX_HEAD_DIM), lambda b, h, qi, ki: (b, kv_block(qi, ki), v_col0 + h)),
                  pl.BlockSpec((None, None, tq, GROUP), lambda b, h, qi, ki: (b, h, qi, 0)),
                  pl.BlockSpec((None, None, GROUP, tk), lambda b, h, qi, ki: (b, h, 0, kv_block(qi, ki)))],
        out_specs=pl.BlockSpec((None, tq, gw), lambda b, h, qi, ki: (b, qi, h)),
        out_shape=jax.ShapeDtypeStruct((bsz, t, n_kv * gw), BF16),
        scratch_shapes=[pltpu.VMEM((GROUP, tq, 1), F32), pltpu.VMEM((GROUP, tq, 1), F32),
                        pltpu.VMEM((GROUP, tq, FOX_HEAD_DIM), F32)],
        compiler_params=_params("parallel", "parallel", "parallel", "arbitrary"),
        name="forgetting_attention",
    )(q_arr, k_arr, v_arr, cq, ck)


def _run_group(x, conv_hist, mem_k, mem_v, past_k, past_v, past_logf, w):
    bsz, t, d = x.shape
    depth = w["norm_g"].shape[0]
    n_a = w["w_in_a"].shape[0]
    mem_width = mem_k.shape[-1]
    mix_width = d - mem_width
    n_q = w["b_f"].shape[0]
    n_kv = n_q // GROUP
    kv_width = n_kv * FOX_HEAD_DIM
    p_len = past_k.shape[1]
    m = bsz * t
    norm_g = w["norm_g"]

    xf = x.reshape(m, d)
    h = norm_cast(xf, norm_g[0, 0].reshape(1, 1, d))[0]
    conv_new = []
    kv_out = None
    for layer in range(depth):
        g = norm_g[layer]
        if layer < n_a:
            glu = matmul_glu(h, w["w_in_a"], layer, mix_width).reshape(bsz, t, mix_width)
            (qm,) = matmul(h[None], w["w_in_a"], layer0=layer, col0=2 * mix_width, out_dtypes=(BF16,))
            hist = conv_hist[layer]
            conv_new.append(glu[:, t - CONV_STATE:] if t >= CONV_STATE
                            else jnp.concatenate([hist[:, t:], glu], axis=1))
            hist_pad = jnp.pad(hist, ((0, 0), (CONV_HALO - CONV_STATE, 0), (0, 0)))
            mix = conv_module(glu, hist_pad, w["conv_w"][layer], w["conv_b"][layer],
                              w["conv_ln_g"][layer], w["conv_ln_b"][layer])
            mem_o = memory_attention(qm[0].reshape(bsz, t, mem_width), 0, mem_k[layer], mem_v[layer])
        else:
            (u,) = matmul(h[None], w["w_in_b"], layer0=layer - n_a, out_dtypes=(BF16,))
            u = u[0].reshape(bsz, t, d)
            k_arr, k_col0, v_arr, v_col0, cq, ck = kv_out
            mix = forgetting_attention(u, k_arr, k_col0, v_arr, v_col0, cq, ck, p_len)
            mem_o = memory_attention(u, mix_width // mem_width, mem_k[layer], mem_v[layer])
        o = matmul_concat(mix.reshape(m, mix_width), mem_o.reshape(m, mem_width), w["w_out"], layer)
        xf, (h2,) = residual_norm(xf, o, g[1], [g[2]])
        (hid,) = matmul(h2[None], w["w_up"], layer0=layer, out_dtypes=(BF16,), act="relu2")
        (f,) = matmul(hid, w["w_down"], layer0=layer, tk=2048)
        next_gains = [norm_g[layer + 1, 0]] if layer + 1 < depth else []
        if layer == n_a - 1:
            next_gains.append(w["kv_norm_g"])
        xf, hs = residual_norm(xf, f[0], g[3], next_gains)
        if layer + 1 < depth:
            h = hs[0]
        if layer == n_a - 1:
            s = hs[-1]
            k_new, k_bf = matmul(s[None], w["w_kv"][None], col0=0, ncols=kv_width, out_dtypes=(F32, BF16))
            v_new, v_bf = matmul(s[None], w["w_kv"][None], col0=kv_width, ncols=kv_width, out_dtypes=(F32, BF16))
            logf_pad = log_forget(s, w["w_f"], w["b_f_pad"]).reshape(bsz, t, LANES)
            logf_new = logf_pad[:, :, :n_q]
            if p_len:
                past_pad = jnp.pad(past_logf.astype(F32), ((0, 0), (0, 0), (0, LANES - n_q)))
                logf_all = jnp.concatenate([past_pad, logf_pad], axis=1)
                k_arr = jnp.concatenate([past_k.reshape(bsz, p_len, kv_width).astype(BF16),
                                         k_bf[0].reshape(bsz, t, kv_width)], axis=1)
                v_arr = jnp.concatenate([past_v.reshape(bsz, p_len, kv_width).astype(BF16),
                                         v_bf[0].reshape(bsz, t, kv_width)], axis=1)
            else:
                logf_all = logf_pad
                k_arr = k_bf[0].reshape(bsz, t, kv_width)
                v_arr = v_bf[0].reshape(bsz, t, kv_width)
            c_all = cumsum_time(logf_all)[:, :, :n_q]
            s_len = p_len + t
            cq = c_all[:, p_len:].reshape(bsz, t, n_kv, GROUP).transpose(0, 2, 1, 3)
            ck = c_all.reshape(bsz, s_len, n_kv, GROUP).transpose(0, 2, 3, 1)
            kv_out = (k_arr, 0, v_arr, 0, cq, ck)
            k_new = k_new[0].reshape(bsz, t, n_kv, FOX_HEAD_DIM)
            v_new = v_new[0].reshape(bsz, t, n_kv, FOX_HEAD_DIM)
    return xf.reshape(bsz, t, d), jnp.stack(conv_new), k_new, v_new, logf_new


def kernel(x_prompt, x_sample, cache_k, cache_v, cache_logf, cache_mem_k, cache_mem_v, state_conv, mem_prompt,
           norm_g, mem_norm_g, w_mem_kv, w_in_a, conv_w, conv_b, conv_ln_g, conv_ln_b, w_in_b, kv_norm_g, w_kv,
           w_f, b_f, w_out, w_up, w_down):
    depth, d = mem_norm_g.shape
    bsz, mtok, _ = mem_prompt.shape
    mem_width = w_mem_kv.shape[2] // 2
    hd = mem_width // MEM_HEADS
    n_q = b_f.shape[0]
    n_a = w_in_a.shape[0]
    w = {
        "norm_g": norm_g, "kv_norm_g": kv_norm_g, "conv_w": conv_w, "conv_b": conv_b,
        "conv_ln_g": conv_ln_g, "conv_ln_b": conv_ln_b, "b_f": b_f,
        "w_in_a": w_in_a.astype(BF16), "w_in_b": w_in_b.astype(BF16), "w_kv": w_kv.astype(BF16),
        "w_out": w_out.astype(BF16), "w_up": w_up.astype(BF16), "w_down": w_down.astype(BF16),
        "w_f": jnp.pad(w_f, ((0, 0), (0, LANES - n_q))).astype(BF16),
        "b_f_pad": jnp.pad(b_f, (0, LANES - n_q)).reshape(1, LANES),
    }

    mem_h = norm_cast(mem_prompt.reshape(bsz * mtok, d), mem_norm_g.reshape(depth, 1, d))
    w_mem = w_mem_kv.astype(BF16)
    mk, mk_bf = matmul(mem_h, w_mem, col0=0, ncols=mem_width, out_dtypes=(F32, BF16))
    mv, mv_bf = matmul(mem_h, w_mem, col0=mem_width, ncols=mem_width, out_dtypes=(F32, BF16))
    mem_k_prompt = mk.reshape(depth, bsz, mtok, MEM_HEADS, hd)
    mem_v_prompt = mv.reshape(depth, bsz, mtok, MEM_HEADS, hd)

    zero_conv = jnp.zeros((n_a, bsz, CONV_STATE, d - mem_width), x_prompt.dtype)
    n_kv = n_q // GROUP
    empty_kv = jnp.zeros((bsz, 0, n_kv, FOX_HEAD_DIM), x_prompt.dtype)
    empty_f = jnp.zeros((bsz, 0, n_q), F32)
    y_prompt, conv_prompt, k_prompt, v_prompt, logf_prompt = _run_group(
        x_prompt, zero_conv, mk_bf.reshape(depth, bsz, mtok, mem_width), mv_bf.reshape(depth, bsz, mtok, mem_width),
        empty_kv, empty_kv, empty_f, w)

    dbsz = x_sample.shape[0]
    y_sample, conv_sample, k_sample, v_sample, logf_sample = _run_group(
        x_sample, state_conv, cache_mem_k.reshape(depth, dbsz, mtok, mem_width).astype(BF16),
        cache_mem_v.reshape(depth, dbsz, mtok, mem_width).astype(BF16), cache_k, cache_v, cache_logf, w)

    return (y_prompt, y_sample, k_prompt, v_prompt, logf_prompt, mem_k_prompt, mem_v_prompt, conv_prompt,
            k_sample, v_sample, logf_sample, conv_sample)
```

```python
import functools

import numpy as np
import jax
import jax.numpy as jnp
from jax import lax
from jax.experimental import pallas as pl
from jax.experimental.pallas import tpu as pltpu

EPS = 1e-6
NEG_INF = -1e30
CONV_WIDTH = 31
CONV_STATE = CONV_WIDTH - 1
CONV_HALO = 32
FOX_HEAD_DIM = 128
GROUP = 3
MEM_HEADS = 4
LANES = 128
SUBLANES = 8
VMEM_LIMIT_BYTES = 56 * 1024 * 1024

BF16 = jnp.bfloat16
F32 = jnp.float32

DOWN_TILES = (dict(tm=1024, tn=1024, tk=4096), dict(tm=2048, tn=1024, tk=2048),
              dict(tm=1024, tn=2048, tk=2048), dict(tm=1024, tn=1024, tk=2048))
FOX_TILES = (dict(tq=512, tk=1024, heads_per_step=4), dict(tq=512, tk=1024, heads_per_step=2))


def _params(*semantics):
    return pltpu.CompilerParams(dimension_semantics=semantics, vmem_limit_bytes=VMEM_LIMIT_BYTES)


def _tile(n, target, mult=8):
    if n <= target:
        return n
    for t in range(target, 0, -1):
        if n % t == 0 and t % mult == 0:
            return t
    return n


def _rms(xf, g):
    return xf * lax.rsqrt(jnp.mean(xf * xf, axis=-1, keepdims=True) + EPS) * g


def _norm_cast_kernel(x_ref, g_ref, o_ref):
    o_ref[...] = _rms(x_ref[...], g_ref[...]).astype(o_ref.dtype)


def norm_cast(x, g):
    m, d = x.shape
    nl = g.shape[0]
    tm = _tile(m, 256)
    return pl.pallas_call(
        _norm_cast_kernel,
        grid=(nl, m // tm),
        in_specs=[pl.BlockSpec((tm, d), lambda l, i: (i, 0)),
                  pl.BlockSpec((None, 1, d), lambda l, i: (l, 0, 0))],
        out_specs=pl.BlockSpec((None, tm, d), lambda l, i: (l, i, 0)),
        out_shape=jax.ShapeDtypeStruct((nl, m, d), BF16),
        compiler_params=_params("parallel", "parallel"),
        name="norm_cast",
    )(x, g)


def _residual_norm_kernel(x_ref, o_ref, gp_ref, *rest, n_next):
    g_refs = rest[:n_next]
    xo_ref = rest[n_next]
    h_refs = rest[n_next + 1:]
    xn = x_ref[...] + _rms(o_ref[...], gp_ref[...])
    xo_ref[...] = xn
    if n_next:
        inv = lax.rsqrt(jnp.mean(xn * xn, axis=-1, keepdims=True) + EPS)
        for g_ref, h_ref in zip(g_refs, h_refs):
            h_ref[...] = (xn * inv * g_ref[...]).astype(h_ref.dtype)


def residual_norm(x, o, g_post, g_next):
    m, d = x.shape
    tm = _tile(m, 256)
    n_next = len(g_next)
    row = pl.BlockSpec((tm, d), lambda i: (i, 0))
    gain = pl.BlockSpec((1, d), lambda i: (0, 0))
    outs = pl.pallas_call(
        functools.partial(_residual_norm_kernel, n_next=n_next),
        grid=(m // tm,),
        in_specs=[row, row, gain] + [gain] * n_next,
        out_specs=[row] + [row] * n_next,
        out_shape=[jax.ShapeDtypeStruct((m, d), F32)] + [jax.ShapeDtypeStruct((m, d), BF16)] * n_next,
        compiler_params=_params("parallel"),
        name="residual_norm",
    )(x, o, g_post.reshape(1, d), *[g.reshape(1, d) for g in g_next])
    return outs[0], list(outs[1:])


def _mm_kernel(x_ref, w_ref, *rest, nk, act, n_out, in_place):
    o_refs = rest[:n_out]
    acc_ref = o_refs[0] if in_place else (rest[n_out] if nk > 1 else None)

    def finish(acc):
        if act == "relu2":
            r = jnp.maximum(acc, 0.0)
            acc = r * r
        for o_ref in o_refs:
            o_ref[...] = acc.astype(o_ref.dtype)

    part = jnp.dot(x_ref[...], w_ref[...], preferred_element_type=F32)
    if nk == 1:
        finish(part)
        return
    k = pl.program_id(3)

    @pl.when(k == 0)
    def _():
        acc_ref[...] = part

    @pl.when(k > 0)
    def _():
        acc_ref[...] += part

    if not in_place:
        @pl.when(k == nk - 1)
        def _():
            finish(acc_ref[...])


def matmul(x, w, *, layer0=0, col0=0, ncols=None, out_dtypes=(F32,), act=None, tm=1024, tn=1024, tk=4096):
    nl, m, kdim = x.shape
    ncols = w.shape[2] - col0 if ncols is None else ncols
    tm, tn, tk = _tile(m, tm), _tile(ncols, tn, LANES), _tile(kdim, tk, LANES)
    nk = kdim // tk
    jb = col0 // tn
    assert col0 % tn == 0 and ncols % tn == 0
    n_out = len(out_dtypes)
    in_place = nk > 1 and act is None and tuple(out_dtypes) == (F32,)
    outs = pl.pallas_call(
        functools.partial(_mm_kernel, nk=nk, act=act, n_out=n_out, in_place=in_place),
        grid=(nl, m // tm, ncols // tn, nk),
        in_specs=[pl.BlockSpec((None, tm, tk), lambda l, i, j, k: (l, i, k)),
                  pl.BlockSpec((None, tk, tn), lambda l, i, j, k: (l + layer0, k, j + jb))],
        out_specs=[pl.BlockSpec((None, tm, tn), lambda l, i, j, k: (l, i, j))] * n_out,
        out_shape=[jax.ShapeDtypeStruct((nl, m, ncols), dt) for dt in out_dtypes],
        scratch_shapes=[pltpu.VMEM((tm, tn), F32)] if nk > 1 and not in_place else [],
        compiler_params=_params("parallel", "parallel", "parallel", "arbitrary"),
        name="matmul_" + (act or "plain"),
    )(x, w)
    return list(outs)


def _glu_kernel(x_ref, wa_ref, wg_ref, o_ref):
    x = x_ref[...]
    a = jnp.dot(x, wa_ref[...], preferred_element_type=F32)
    gate = jnp.dot(x, wg_ref[...], preferred_element_type=F32)
    o_ref[...] = a * (1.0 / (1.0 + jnp.exp(-gate)))


def matmul_glu(x, w, layer, width, *, tm=1024, tn=512):
    m, kdim = x.shape
    tm, tn = _tile(m, tm), _tile(width, tn, LANES)
    nj = width // tn
    return pl.pallas_call(
        _glu_kernel,
        grid=(m // tm, nj),
        in_specs=[pl.BlockSpec((tm, kdim), lambda i, j: (i, 0)),
                  pl.BlockSpec((None, kdim, tn), lambda i, j: (layer, 0, j)),
                  pl.BlockSpec((None, kdim, tn), lambda i, j: (layer, 0, j + nj))],
        out_specs=pl.BlockSpec((tm, tn), lambda i, j: (i, j)),
        out_shape=jax.ShapeDtypeStruct((m, width), F32),
        compiler_params=_params("parallel", "parallel"),
        name="matmul_glu",
    )(x, w, w)


def _mm2_kernel(x1_ref, x2_ref, w1_ref, w2_ref, o_ref):
    o_ref[...] = (jnp.dot(x1_ref[...], w1_ref[...], preferred_element_type=F32)
                  + jnp.dot(x2_ref[...], w2_ref[...], preferred_element_type=F32))


def matmul_concat(x1, x2, w, layer, *, tm=1024, tn=1024):
    m, k1 = x1.shape
    k2 = x2.shape[1]
    n = w.shape[2]
    assert k1 % k2 == 0
    tm, tn = _tile(m, tm), _tile(n, tn, LANES)
    return pl.pallas_call(
        _mm2_kernel,
        grid=(m // tm, n // tn),
        in_specs=[pl.BlockSpec((tm, k1), lambda i, j: (i, 0)),
                  pl.BlockSpec((tm, k2), lambda i, j: (i, 0)),
                  pl.BlockSpec((None, k1, tn), lambda i, j: (layer, 0, j)),
                  pl.BlockSpec((None, k2, tn), lambda i, j: (layer, k1 // k2, j))],
        out_specs=pl.BlockSpec((tm, tn), lambda i, j: (i, j)),
        out_shape=jax.ShapeDtypeStruct((m, n), F32),
        compiler_params=_params("parallel", "parallel"),
        name="matmul_concat",
    )(x1, x2, w, w)


def _logf_kernel(x_ref, w_ref, b_ref, o_ref):
    z = jnp.dot(x_ref[...], w_ref[...], preferred_element_type=F32) + b_ref[...]
    o_ref[...] = jnp.minimum(z, 0.0) - jnp.log(1.0 + jnp.exp(-jnp.abs(z)))


def log_forget(s, w_f, b_f):
    m, d = s.shape
    n = w_f.shape[1]
    tm = _tile(m, 1024)
    return pl.pallas_call(
        _logf_kernel,
        grid=(m // tm,),
        in_specs=[pl.BlockSpec((tm, d), lambda i: (i, 0)),
                  pl.BlockSpec((d, n), lambda i: (0, 0)),
                  pl.BlockSpec((1, n), lambda i: (0, 0))],
        out_specs=pl.BlockSpec((tm, n), lambda i: (i, 0)),
        out_shape=jax.ShapeDtypeStruct((m, n), F32),
        compiler_params=_params("parallel"),
        name="log_forget",
    )(s, w_f, b_f)


def _split3(x):
    x1 = x.astype(BF16)
    r1 = x - x1.astype(F32)
    x2 = r1.astype(BF16)
    x3 = (r1 - x2.astype(F32)).astype(BF16)
    return x1, x2, x3


def _forget_bias_kernel(x_ref, pk_ref, pq_ref, kb_ref, qb_ref, carry_ref, *, ts):
    @pl.when(pl.program_id(1) == 0)
    def _():
        carry_ref[...] = jnp.zeros_like(carry_ref)

    row = lax.broadcasted_iota(jnp.int32, (ts, ts), 0)
    col = lax.broadcasted_iota(jnp.int32, (ts, ts), 1)
    tri = (col <= row).astype(BF16)
    y = carry_ref[...]
    for part in _split3(x_ref[...]):
        y = y + jnp.dot(tri, part, preferred_element_type=F32)
    carry_ref[...] = y[ts - 1:ts, :]

    z1, z2, z3 = _split3(y * (FOX_HEAD_DIM ** 0.5))
    lane = lax.broadcasted_iota(jnp.int32, z1.shape, 1)
    z1 = jnp.where(lane == LANES - 1, 1.0, z1).astype(BF16)
    kb = qb = None
    for j, z in enumerate((z1, z2, z3)):
        dk = jnp.dot(z, pk_ref[j], preferred_element_type=F32)
        dq = jnp.dot(z, pq_ref[j], preferred_element_type=F32)
        kb = dk if kb is None else kb + dk
        qb = dq if qb is None else qb + dq
    kb_ref[...] = kb.astype(kb_ref.dtype)
    qb_ref[...] = qb.astype(qb_ref.dtype)


def _bias_placement(n_kv):
    n_q = n_kv * GROUP
    assert n_q < LANES and 4 * GROUP <= LANES
    pk = np.zeros((3, LANES, n_kv * LANES), np.float32)
    pq = np.zeros((3, LANES, n_q * LANES), np.float32)
    for h in range(n_kv):
        for g in range(GROUP):
            hq = h * GROUP + g
            for j in range(3):
                pk[j, hq, h * LANES + GROUP * j + g] = -1.0
                pq[j, hq, hq * LANES + GROUP * 3 + j] = 1.0
                pq[0, LANES - 1, hq * LANES + GROUP * j + g] = 1.0
        for j in range(3):
            pk[0, LANES - 1, h * LANES + GROUP * 3 + j] = 1.0
    return jnp.asarray(pk, BF16), jnp.asarray(pq, BF16)


def forget_bias(logf, n_kv):
    b, s, n = logf.shape
    s_pad = -(-s // LANES) * LANES
    if s_pad != s:
        logf = jnp.pad(logf, ((0, 0), (0, s_pad - s), (0, 0)))
    ts = _tile(s_pad, 1024, LANES)
    pk, pq = _bias_placement(n_kv)
    nk_l, nq_l = pk.shape[2], pq.shape[2]
    return pl.pallas_call(
        functools.partial(_forget_bias_kernel, ts=ts),
        grid=(b, s_pad // ts),
        in_specs=[pl.BlockSpec((None, ts, n), lambda bi, i: (bi, i, 0)),
                  pl.BlockSpec((3, n, nk_l), lambda bi, i: (0, 0, 0)),
                  pl.BlockSpec((3, n, nq_l), lambda bi, i: (0, 0, 0))],
        out_specs=[pl.BlockSpec((None, ts, nk_l), lambda bi, i: (bi, i, 0)),
                   pl.BlockSpec((None, ts, nq_l), lambda bi, i: (bi, i, 0))],
        out_shape=[jax.ShapeDtypeStruct((b, s_pad, nk_l), BF16), jax.ShapeDtypeStruct((b, s_pad, nq_l), BF16)],
        scratch_shapes=[pltpu.VMEM((1, n), F32)],
        compiler_params=_params("parallel", "arbitrary"),
        name="forget_bias",
    )(logf, pk, pq)


def _conv_kernel(*refs, tt, has_prev):
    if has_prev:
        hist_ref, prev_ref, cur_ref, w_ref, b_ref, g_ref, lb_ref, o_ref, xs_ref, cb_ref = refs
    else:
        hist_ref, cur_ref, w_ref, b_ref, g_ref, lb_ref, o_ref, xs_ref, cb_ref = refs
    i = pl.program_id(1)

    @pl.when(i == 0)
    def _():
        xs_ref[0:CONV_HALO, :] = hist_ref[...]

    if has_prev:
        @pl.when(i > 0)
        def _():
            xs_ref[0:CONV_HALO, :] = prev_ref[...]

    xs_ref[CONV_HALO:CONV_HALO + tt, :] = cur_ref[...]
    first = CONV_HALO - CONV_STATE
    rows = CONV_HALO + tt
    for c0 in range(0, xs_ref.shape[1], LANES):
        lanes = pl.ds(c0, LANES)
        window = xs_ref[:, lanes]
        acc = None
        for r in range(SUBLANES):
            offsets = [o for o in range(r, CONV_HALO + 1, SUBLANES) if first <= o < first + CONV_WIDTH]
            if not offsets:
                continue
            shifted = pltpu.roll(window, rows - r, axis=0) if r else window
            for o in offsets:
                k = o - first
                term = w_ref[k:k + 1, lanes] * shifted[o - r:o - r + tt]
                acc = term if acc is None else acc + term
        cb_ref[:, lanes] = acc + b_ref[:, lanes]
    c = cb_ref[...]
    mu = jnp.mean(c, axis=-1, keepdims=True)
    xc = c - mu
    var = jnp.mean(xc * xc, axis=-1, keepdims=True)
    y = xc * lax.rsqrt(var + EPS) * g_ref[...] + lb_ref[...]
    o_ref[...] = (y * (1.0 / (1.0 + jnp.exp(-y)))).astype(o_ref.dtype)


def conv_module(glu, hist, w, b, ln_g, ln_b, *, tt=128):
    bsz, t, c = glu.shape
    tt = _tile(t, tt, CONV_HALO) if t > tt else t
    nt = t // tt
    has_prev = nt > 1
    r = tt // CONV_HALO
    vec = pl.BlockSpec((1, c), lambda bi, i: (0, 0))
    in_specs = [pl.BlockSpec((None, CONV_HALO, c), lambda bi, i: (bi, 0, 0))]
    args = [hist]
    if has_prev:
        in_specs.append(pl.BlockSpec((None, CONV_HALO, c), lambda bi, i: (bi, jnp.maximum(i * r - 1, 0), 0)))
        args.append(glu)
    in_specs += [pl.BlockSpec((None, tt, c), lambda bi, i: (bi, i, 0)),
                 pl.BlockSpec((CONV_WIDTH, c), lambda bi, i: (0, 0)), vec, vec, vec]
    args += [glu, w, b.reshape(1, c), ln_g.reshape(1, c), ln_b.reshape(1, c)]
    return pl.pallas_call(
        functools.partial(_conv_kernel, tt=tt, has_prev=has_prev),
        grid=(bsz, nt),
        in_specs=in_specs,
        out_specs=pl.BlockSpec((None, tt, c), lambda bi, i: (bi, i, 0)),
        out_shape=jax.ShapeDtypeStruct((bsz, t, c), BF16),
        scratch_shapes=[pltpu.VMEM((CONV_HALO + tt, c), F32), pltpu.VMEM((tt, c), F32)],
        compiler_params=_params("parallel", "arbitrary"),
        name="conv_module",
    )(*args)


def _mem_attn_kernel(q_ref, k_ref, v_ref, o_ref, *, hd):
    scale = hd ** -0.5
    for h in range(MEM_HEADS):
        sl = slice(h * hd, (h + 1) * hd)
        s = lax.dot_general(q_ref[:, sl], k_ref[:, sl], (((1,), (1,)), ((), ())),
                            preferred_element_type=F32) * scale
        m = jnp.max(s, axis=-1, keepdims=True)
        p = jnp.exp(s - m)
        p = p / jnp.sum(p, axis=-1, keepdims=True)
        o_ref[:, sl] = jnp.dot(p.astype(BF16), v_ref[:, sl], preferred_element_type=F32).astype(o_ref.dtype)


def memory_attention(q_arr, q_col_block, mk, mv, *, tt=512):
    bsz, t, _ = q_arr.shape
    mtok, width = mk.shape[1:]
    tt = _tile(t, tt)
    return pl.pallas_call(
        functools.partial(_mem_attn_kernel, hd=width // MEM_HEADS),
        grid=(bsz, t // tt),
        in_specs=[pl.BlockSpec((None, tt, width), lambda bi, i: (bi, i, q_col_block)),
                  pl.BlockSpec((None, mtok, width), lambda bi, i: (bi, 0, 0)),
                  pl.BlockSpec((None, mtok, width), lambda bi, i: (bi, 0, 0))],
        out_specs=pl.BlockSpec((None, tt, width), lambda bi, i: (bi, i, 0)),
        out_shape=jax.ShapeDtypeStruct((bsz, t, width), BF16),
        compiler_params=_params("parallel", "parallel"),
        name="memory_attention",
    )(q_arr, mk, mv)


def _fox_kernel(q_ref, qb_ref, k_ref, kb_ref, v_ref, o_ref, m_ref, acc_ref, *, tq, tk, p_len, hps):
    qi = pl.program_id(2)
    ki = pl.program_id(3)
    dh = FOX_HEAD_DIM
    log2e_scale = (dh ** -0.5) * 1.4426950408889634

    @pl.when(ki == 0)
    def _():
        m_ref[...] = jnp.full_like(m_ref, NEG_INF)
        acc_ref[...] = jnp.zeros_like(acc_ref)

    q_first = p_len + qi * tq
    k_first = ki * tk
    needed = k_first <= q_first + tq - 1
    crosses_diagonal = k_first + tk - 1 > q_first

    def step(masked):
        if masked:
            q_pos = q_first + lax.broadcasted_iota(jnp.int32, (tq, tk), 0)
            k_pos = k_first + lax.broadcasted_iota(jnp.int32, (tq, tk), 1)
            visible = k_pos <= q_pos
        for hh in range(hps):
            ks = slice(hh * dh, (hh + 1) * dh)
            k_aug = jnp.concatenate([k_ref[:, ks], kb_ref[:, ks]], axis=1)
            v_aug = jnp.concatenate([v_ref[:, ks], jnp.ones((tk, dh), BF16)], axis=1)
            for g in range(GROUP):
                hq = hh * GROUP + g
                sl = slice(hq * dh, (hq + 1) * dh)
                q_aug = jnp.concatenate([q_ref[:, sl], qb_ref[:, sl]], axis=1)
                r = lax.dot_general(q_aug, k_aug, (((1,), (1,)), ((), ())), preferred_element_type=F32)
                if masked:
                    r = jnp.where(visible, r, NEG_INF)
                m_prev = m_ref[hq]
                m_new = jnp.maximum(m_prev, jnp.max(r, axis=-1, keepdims=True))
                alpha = jnp.exp2((m_prev - m_new) * log2e_scale)
                p = jnp.exp2((r - m_new) * log2e_scale)
                acc_ref[hq] = alpha * acc_ref[hq] + jnp.dot(p.astype(BF16), v_aug, preferred_element_type=F32)
                m_ref[hq] = m_new

    @pl.when(jnp.logical_and(needed, jnp.logical_not(crosses_diagonal)))
    def _():
        step(False)

    @pl.when(jnp.logical_and(needed, crosses_diagonal))
    def _():
        step(True)

    @pl.when(ki == pl.num_programs(3) - 1)
    def _():
        for hq in range(hps * GROUP):
            o_ref[:, hq * dh:(hq + 1) * dh] = (acc_ref[hq, :, 0:dh] / acc_ref[hq, :, dh:2 * dh]).astype(o_ref.dtype)


def forgetting_attention(q_arr, qb, k_arr, kb, v_arr, p_len, *, tq=512, tk=1024, heads_per_step=4):
    bsz, t, _ = q_arr.shape
    s_len = k_arr.shape[1]
    dh = FOX_HEAD_DIM
    n_kv = k_arr.shape[2] // dh
    tq, tk = _tile(t, tq), _tile(s_len, tk, LANES)
    hps = _tile(n_kv, heads_per_step, 1)
    gw = hps * GROUP * dh

    def kv_block(qi, ki):
        return jnp.minimum(ki, (p_len + (qi + 1) * tq - 1) // tk)

    q_spec = pl.BlockSpec((None, tq, gw), lambda b, h, qi, ki: (b, qi, h))
    kv_spec = pl.BlockSpec((None, tk, hps * dh), lambda b, h, qi, ki: (b, kv_block(qi, ki), h))
    return pl.pallas_call(
        functools.partial(_fox_kernel, tq=tq, tk=tk, p_len=p_len, hps=hps),
        grid=(bsz, n_kv // hps, t // tq, s_len // tk),
        in_specs=[q_spec, q_spec, kv_spec, kv_spec, kv_spec],
        out_specs=q_spec,
        out_shape=jax.ShapeDtypeStruct((bsz, t, n_kv * GROUP * dh), BF16),
        scratch_shapes=[pltpu.VMEM((hps * GROUP, tq, 1), F32), pltpu.VMEM((hps * GROUP, tq, 2 * dh), F32)],
        compiler_params=_params("parallel", "parallel", "parallel", "arbitrary"),
        name="forgetting_attention",
    )(q_arr, qb, k_arr, kb, v_arr)


def _run_group(x, conv_hist, mem_k, mem_v, past_k, past_v, past_logf, w):
    bsz, t, d = x.shape
    depth = w["norm_g"].shape[0]
    n_a = w["w_in_a"].shape[0]
    mem_width = mem_k.shape[-1]
    mix_width = d - mem_width
    n_q = w["b_f"].shape[0]
    n_kv = n_q // GROUP
    kv_width = n_kv * FOX_HEAD_DIM
    p_len = past_k.shape[1]
    m = bsz * t
    norm_g = w["norm_g"]

    xf = x.reshape(m, d)
    h = norm_cast(xf, norm_g[0, 0].reshape(1, 1, d))[0]
    conv_new = []
    kv_out = None
    for layer in range(depth):
        g = norm_g[layer]
        if layer < n_a:
            glu = matmul_glu(h, w["w_in_a"], layer, mix_width).reshape(bsz, t, mix_width)
            (qm,) = matmul(h[None], w["w_in_a"], layer0=layer, col0=2 * mix_width, out_dtypes=(BF16,))
            hist = conv_hist[layer]
            conv_new.append(glu[:, t - CONV_STATE:] if t >= CONV_STATE
                            else jnp.concatenate([hist[:, t:], glu], axis=1))
            hist_pad = jnp.pad(hist, ((0, 0), (CONV_HALO - CONV_STATE, 0), (0, 0)))
            mix = conv_module(glu, hist_pad, w["conv_w"][layer], w["conv_b"][layer],
                              w["conv_ln_g"][layer], w["conv_ln_b"][layer])
            mem_o = memory_attention(qm[0].reshape(bsz, t, mem_width), 0, mem_k[layer], mem_v[layer])
        else:
            (u,) = matmul(h[None], w["w_in_b"], layer0=layer - n_a, out_dtypes=(BF16,))
            u = u[0].reshape(bsz, t, d)
            qb, k_arr, kb, v_arr = kv_out
            mix = forgetting_attention(u, qb, k_arr, kb, v_arr, p_len, **FOX_TILES[layer % len(FOX_TILES)])
            mem_o = memory_attention(u, mix_width // mem_width, mem_k[layer], mem_v[layer])
        o = matmul_concat(mix.reshape(m, mix_width), mem_o.reshape(m, mem_width), w["w_out"], layer)
        xf, (h2,) = residual_norm(xf, o, g[1], [g[2]])
        (hid,) = matmul(h2[None], w["w_up"], layer0=layer, out_dtypes=(BF16,), act="relu2")
        (f,) = matmul(hid, w["w_down"], layer0=layer, **DOWN_TILES[layer % len(DOWN_TILES)])
        next_gains = [norm_g[layer + 1, 0]] if layer + 1 < depth else []
        if layer == n_a - 1:
            next_gains.append(w["kv_norm_g"])
        xf, hs = residual_norm(xf, f[0], g[3], next_gains)
        if layer + 1 < depth:
            h = hs[0]
        if layer == n_a - 1:
            s = hs[-1]
            k_new, k_bf = matmul(s[None], w["w_kv"][None], col0=0, ncols=kv_width, out_dtypes=(F32, BF16))
            v_new, v_bf = matmul(s[None], w["w_kv"][None], col0=kv_width, ncols=kv_width, out_dtypes=(F32, BF16))
            logf_pad = log_forget(s, w["w_f"], w["b_f_pad"]).reshape(bsz, t, LANES)
            logf_new = logf_pad[:, :, :n_q]
            if p_len:
                past_pad = jnp.pad(past_logf.astype(F32), ((0, 0), (0, 0), (0, LANES - n_q)))
                logf_all = jnp.concatenate([past_pad, logf_pad], axis=1)
                k_arr = jnp.concatenate([past_k.reshape(bsz, p_len, kv_width).astype(BF16),
                                         k_bf[0].reshape(bsz, t, kv_width)], axis=1)
                v_arr = jnp.concatenate([past_v.reshape(bsz, p_len, kv_width).astype(BF16),
                                         v_bf[0].reshape(bsz, t, kv_width)], axis=1)
            else:
                logf_all = logf_pad
                k_arr = k_bf[0].reshape(bsz, t, kv_width)
                v_arr = v_bf[0].reshape(bsz, t, kv_width)
            kb, qb = forget_bias(logf_all, n_kv)
            s_len = p_len + t
            if kb.shape[1] != s_len:
                kb = kb[:, :s_len]
            if qb.shape[1] != t:
                qb = qb[:, p_len:s_len]
            kv_out = (qb, k_arr, kb, v_arr)
            k_new = k_new[0].reshape(bsz, t, n_kv, FOX_HEAD_DIM)
            v_new = v_new[0].reshape(bsz, t, n_kv, FOX_HEAD_DIM)
    return xf.reshape(bsz, t, d), jnp.stack(conv_new), k_new, v_new, logf_new


def kernel(x_prompt, x_sample, cache_k, cache_v, cache_logf, cache_mem_k, cache_mem_v, state_conv, mem_prompt,
           norm_g, mem_norm_g, w_mem_kv, w_in_a, conv_w, conv_b, conv_ln_g, conv_ln_b, w_in_b, kv_norm_g, w_kv,
           w_f, b_f, w_out, w_up, w_down):
    depth, d = mem_norm_g.shape
    bsz, mtok, _ = mem_prompt.shape
    mem_width = w_mem_kv.shape[2] // 2
    hd = mem_width // MEM_HEADS
    n_q = b_f.shape[0]
    n_a = w_in_a.shape[0]
    w = {
        "norm_g": norm_g, "kv_norm_g": kv_norm_g, "conv_w": conv_w, "conv_b": conv_b,
        "conv_ln_g": conv_ln_g, "conv_ln_b": conv_ln_b, "b_f": b_f,
        "w_in_a": w_in_a.astype(BF16), "w_in_b": w_in_b.astype(BF16), "w_kv": w_kv.astype(BF16),
        "w_out": w_out.astype(BF16), "w_up": w_up.astype(BF16), "w_down": w_down.astype(BF16),
        "w_f": jnp.pad(w_f, ((0, 0), (0, LANES - n_q))).astype(BF16),
        "b_f_pad": jnp.pad(b_f, (0, LANES - n_q)).reshape(1, LANES),
    }

    mem_h = norm_cast(mem_prompt.reshape(bsz * mtok, d), mem_norm_g.reshape(depth, 1, d))
    w_mem = w_mem_kv.astype(BF16)
    mk, mk_bf = matmul(mem_h, w_mem, col0=0, ncols=mem_width, out_dtypes=(F32, BF16))
    mv, mv_bf = matmul(mem_h, w_mem, col0=mem_width, ncols=mem_width, out_dtypes=(F32, BF16))
    mem_k_prompt = mk.reshape(depth, bsz, mtok, MEM_HEADS, hd)
    mem_v_prompt = mv.reshape(depth, bsz, mtok, MEM_HEADS, hd)

    zero_conv = jnp.zeros((n_a, bsz, CONV_STATE, d - mem_width), x_prompt.dtype)
    n_kv = n_q // GROUP
    empty_kv = jnp.zeros((bsz, 0, n_kv, FOX_HEAD_DIM), x_prompt.dtype)
    empty_f = jnp.zeros((bsz, 0, n_q), F32)
    y_prompt, conv_prompt, k_prompt, v_prompt, logf_prompt = _run_group(
        x_prompt, zero_conv, mk_bf.reshape(depth, bsz, mtok, mem_width), mv_bf.reshape(depth, bsz, mtok, mem_width),
        empty_kv, empty_kv, empty_f, w)

    dbsz = x_sample.shape[0]
    y_sample, conv_sample, k_sample, v_sample, logf_sample = _run_group(
        x_sample, state_conv, cache_mem_k.reshape(depth, dbsz, mtok, mem_width).astype(BF16),
        cache_mem_v.reshape(depth, dbsz, mtok, mem_width).astype(BF16), cache_k, cache_v, cache_logf, w)

    return (y_prompt, y_sample, k_prompt, v_prompt, logf_prompt, mem_k_prompt, mem_v_prompt, conv_prompt,
            k_sample, v_sample, logf_sample, conv_sample)
```

```python
import functools

import numpy as np
import jax
import jax.numpy as jnp
from jax import lax
from jax.experimental import pallas as pl
from jax.experimental.pallas import tpu as pltpu

EPS = 1e-6
NEG_INF = -1e30
CONV_WIDTH = 31
CONV_STATE = CONV_WIDTH - 1
CONV_HALO = 32
FOX_HEAD_DIM = 128
GROUP = 3
MEM_HEADS = 4
LANES = 128
SUBLANES = 8
VMEM_LIMIT_BYTES = 56 * 1024 * 1024

BF16 = jnp.bfloat16
F32 = jnp.float32


def _params(*semantics):
    return pltpu.CompilerParams(dimension_semantics=semantics, vmem_limit_bytes=VMEM_LIMIT_BYTES)


def _tile(n, target, mult=8):
    if n <= target:
        return n
    for t in range(target, 0, -1):
        if n % t == 0 and t % mult == 0:
            return t
    return n


def _rms(xf, g):
    return xf * lax.rsqrt(jnp.mean(xf * xf, axis=-1, keepdims=True) + EPS) * g


def _norm_cast_kernel(x_ref, g_ref, o_ref):
    o_ref[...] = _rms(x_ref[...], g_ref[...]).astype(o_ref.dtype)


def norm_cast(x, g):
    m, d = x.shape
    nl = g.shape[0]
    tm = _tile(m, 256)
    return pl.pallas_call(
        _norm_cast_kernel,
        grid=(nl, m // tm),
        in_specs=[pl.BlockSpec((tm, d), lambda l, i: (i, 0)),
                  pl.BlockSpec((None, 1, d), lambda l, i: (l, 0, 0))],
        out_specs=pl.BlockSpec((None, tm, d), lambda l, i: (l, i, 0)),
        out_shape=jax.ShapeDtypeStruct((nl, m, d), BF16),
        compiler_params=_params("parallel", "parallel"),
        name="norm_cast",
    )(x, g)


def _residual_norm_kernel(x_ref, o_ref, gp_ref, *rest, n_next):
    g_refs = rest[:n_next]
    xo_ref = rest[n_next]
    h_refs = rest[n_next + 1:]
    xn = x_ref[...] + _rms(o_ref[...], gp_ref[...])
    xo_ref[...] = xn
    if n_next:
        inv = lax.rsqrt(jnp.mean(xn * xn, axis=-1, keepdims=True) + EPS)
        for g_ref, h_ref in zip(g_refs, h_refs):
            h_ref[...] = (xn * inv * g_ref[...]).astype(h_ref.dtype)


def residual_norm(x, o, g_post, g_next):
    m, d = x.shape
    tm = _tile(m, 256)
    n_next = len(g_next)
    row = pl.BlockSpec((tm, d), lambda i: (i, 0))
    gain = pl.BlockSpec((1, d), lambda i: (0, 0))
    outs = pl.pallas_call(
        functools.partial(_residual_norm_kernel, n_next=n_next),
        grid=(m // tm,),
        in_specs=[row, row, gain] + [gain] * n_next,
        out_specs=[row] + [row] * n_next,
        out_shape=[jax.ShapeDtypeStruct((m, d), F32)] + [jax.ShapeDtypeStruct((m, d), BF16)] * n_next,
        compiler_params=_params("parallel"),
        name="residual_norm",
    )(x, o, g_post.reshape(1, d), *[g.reshape(1, d) for g in g_next])
    return outs[0], list(outs[1:])


def _weight_tile(w_ref, wb_ref):
    w = w_ref[...].astype(BF16)
    if wb_ref is not None:
        wb_ref[...] = w
    return w


def _mm_kernel(x_ref, w_ref, *rest, nk, act, n_out, in_place, emit):
    o_refs = rest[:n_out]
    wb_ref = rest[n_out] if emit else None
    acc_ref = o_refs[0] if in_place else (rest[n_out + emit] if nk > 1 else None)

    def finish(acc):
        if act == "relu2":
            r = jnp.maximum(acc, 0.0)
            acc = r * r
        for o_ref in o_refs:
            o_ref[...] = acc.astype(o_ref.dtype)

    part = jnp.dot(x_ref[...], _weight_tile(w_ref, wb_ref), preferred_element_type=F32)
    if nk == 1:
        finish(part)
        return
    k = pl.program_id(3)

    @pl.when(k == 0)
    def _():
        acc_ref[...] = part

    @pl.when(k > 0)
    def _():
        acc_ref[...] += part

    if not in_place:
        @pl.when(k == nk - 1)
        def _():
            finish(acc_ref[...])


def _mm_row_panel_kernel(x_ref, w_ref, o_ref, acc_ref, *, nk):
    k = pl.program_id(2)
    j = pl.program_id(3)
    part = jnp.dot(x_ref[...], w_ref[...], preferred_element_type=F32)

    @pl.when(k == 0)
    def _():
        acc_ref[j] = part

    @pl.when(jnp.logical_and(k > 0, k < nk - 1))
    def _():
        acc_ref[j] += part

    @pl.when(jnp.logical_and(k > 0, k == nk - 1))
    def _():
        o_ref[...] = acc_ref[j] + part


def matmul_row_panel(x, wd, *, tm=1024, tn=1024, tk=2048):
    w, layer0, col0 = wd
    nl, m, kdim = x.shape
    n = w.shape[2]
    assert nl == 1 and col0 == 0 and w.dtype == BF16
    tm, tn, tk = _tile(m, tm), _tile(n, tn, LANES), _tile(kdim, tk, LANES)
    nk, nj = kdim // tk, n // tn
    assert nk >= 2
    return pl.pallas_call(
        functools.partial(_mm_row_panel_kernel, nk=nk),
        grid=(nl, m // tm, nk, nj),
        in_specs=[pl.BlockSpec((None, tm, tk), lambda l, i, k, j: (l, i, k)),
                  pl.BlockSpec((None, tk, tn), lambda l, i, k, j: (l + layer0, k, j))],
        out_specs=pl.BlockSpec((None, tm, tn), lambda l, i, k, j: (l, i, jnp.where(k == nk - 1, j, 0))),
        out_shape=jax.ShapeDtypeStruct((nl, m, n), F32),
        scratch_shapes=[pltpu.VMEM((nj, tm, tn), F32)],
        compiler_params=_params("parallel", "parallel", "arbitrary", "arbitrary"),
        name="matmul_row_panel",
    )(x, w)


EMIT_TN = 512


def matmul(x, wd, *, ncols=None, out_dtypes=(F32,), act=None, emit=False, tm=1024, tn=1024, tk=4096):
    w, layer0, col0 = wd
    nl, m, kdim = x.shape
    ncols = w.shape[2] - col0 if ncols is None else ncols
    tn = min(tn, EMIT_TN) if w.dtype == F32 else tn
    tm, tn, tk = _tile(m, tm), _tile(ncols, tn, LANES), _tile(kdim, tk, LANES)
    nk = kdim // tk
    jb = col0 // tn
    assert col0 % tn == 0 and ncols % tn == 0
    assert not emit or (nl == 1 and m == tm)
    n_out = len(out_dtypes)
    in_place = nk > 1 and act is None and tuple(out_dtypes) == (F32,)
    out_specs = [pl.BlockSpec((None, tm, tn), lambda l, i, j, k: (l, i, j))] * n_out
    out_shape = [jax.ShapeDtypeStruct((nl, m, ncols), dt) for dt in out_dtypes]
    if emit:
        out_specs.append(pl.BlockSpec((None, tk, tn), lambda l, i, j, k: (0, k, j)))
        out_shape.append(jax.ShapeDtypeStruct((1, kdim, ncols), BF16))
    outs = pl.pallas_call(
        functools.partial(_mm_kernel, nk=nk, act=act, n_out=n_out, in_place=in_place, emit=emit),
        grid=(nl, m // tm, ncols // tn, nk),
        in_specs=[pl.BlockSpec((None, tm, tk), lambda l, i, j, k: (l, i, k)),
                  pl.BlockSpec((None, tk, tn), lambda l, i, j, k: (l + layer0, k, j + jb))],
        out_specs=out_specs,
        out_shape=out_shape,
        scratch_shapes=[pltpu.VMEM((tm, tn), F32)] if nk > 1 and not in_place else [],
        compiler_params=_params("parallel", "parallel", "parallel", "arbitrary"),
        name="matmul_" + (act or "plain"),
    )(x, w)
    return list(outs)


def _glu_kernel(x_ref, wa_ref, wg_ref, o_ref, *wb_refs):
    x = x_ref[...]
    wba_ref, wbg_ref = wb_refs if wb_refs else (None, None)
    a = jnp.dot(x, _weight_tile(wa_ref, wba_ref), preferred_element_type=F32)
    gate = jnp.dot(x, _weight_tile(wg_ref, wbg_ref), preferred_element_type=F32)
    o_ref[...] = a * (1.0 / (1.0 + jnp.exp(-gate)))


def matmul_glu(x, wd_a, wd_g, width, *, emit=False, tm=1024, tn=512):
    m, kdim = x.shape
    (wa, la, ca), (wg, lg, cg) = wd_a, wd_g
    tn = min(tn, EMIT_TN // 2) if wa.dtype == F32 else tn
    tm, tn = _tile(m, tm), _tile(width, tn, LANES)
    assert ca % tn == 0 and cg % tn == 0 and (not emit or m == tm)
    ja, jg = ca // tn, cg // tn
    out_specs = [pl.BlockSpec((tm, tn), lambda i, j: (i, j))]
    out_shape = [jax.ShapeDtypeStruct((m, width), F32)]
    if emit:
        out_specs += [pl.BlockSpec((None, kdim, tn), lambda i, j: (0, 0, j))] * 2
        out_shape += [jax.ShapeDtypeStruct((1, kdim, width), BF16)] * 2
    outs = pl.pallas_call(
        _glu_kernel,
        grid=(m // tm, width // tn),
        in_specs=[pl.BlockSpec((tm, kdim), lambda i, j: (i, 0)),
                  pl.BlockSpec((None, kdim, tn), lambda i, j: (la, 0, j + ja)),
                  pl.BlockSpec((None, kdim, tn), lambda i, j: (lg, 0, j + jg))],
        out_specs=out_specs,
        out_shape=out_shape,
        compiler_params=_params("parallel", "parallel"),
        name="matmul_glu",
    )(x, wa, wg)
    return list(outs)


def _mm2_kernel(x1_ref, x2_ref, w1_ref, w2_ref, o_ref, *wb_refs):
    wb1_ref, wb2_ref = wb_refs if wb_refs else (None, None)
    o_ref[...] = (jnp.dot(x1_ref[...], _weight_tile(w1_ref, wb1_ref), preferred_element_type=F32)
                  + jnp.dot(x2_ref[...], _weight_tile(w2_ref, wb2_ref), preferred_element_type=F32))


def matmul_concat(x1, x2, wd1, wd2, *, emit=False, tm=1024, tn=1024):
    m, k1 = x1.shape
    k2 = x2.shape[1]
    (w1, l1, r1), (w2, l2, r2) = wd1, wd2
    n = w1.shape[2]
    assert r1 % k1 == 0 and r2 % k2 == 0
    tn = min(tn, EMIT_TN) if w1.dtype == F32 else tn
    tm, tn = _tile(m, tm), _tile(n, tn, LANES)
    assert not emit or m == tm
    out_specs = [pl.BlockSpec((tm, tn), lambda i, j: (i, j))]
    out_shape = [jax.ShapeDtypeStruct((m, n), F32)]
    if emit:
        out_specs += [pl.BlockSpec((None, k1, tn), lambda i, j: (0, 0, j)),
                      pl.BlockSpec((None, k2, tn), lambda i, j: (0, 0, j))]
        out_shape += [jax.ShapeDtypeStruct((1, k1, n), BF16), jax.ShapeDtypeStruct((1, k2, n), BF16)]
    outs = pl.pallas_call(
        _mm2_kernel,
        grid=(m // tm, n // tn),
        in_specs=[pl.BlockSpec((tm, k1), lambda i, j: (i, 0)),
                  pl.BlockSpec((tm, k2), lambda i, j: (i, 0)),
                  pl.BlockSpec((None, k1, tn), lambda i, j: (l1, r1 // k1, j)),
                  pl.BlockSpec((None, k2, tn), lambda i, j: (l2, r2 // k2, j))],
        out_specs=out_specs,
        out_shape=out_shape,
        compiler_params=_params("parallel", "parallel"),
        name="matmul_concat",
    )(x1, x2, w1, w2)
    return list(outs)


def _logf_kernel(x_ref, w_ref, b_ref, o_ref):
    z = jnp.dot(x_ref[...], w_ref[...], preferred_element_type=F32) + b_ref[...]
    o_ref[...] = jnp.minimum(z, 0.0) - jnp.log(1.0 + jnp.exp(-jnp.abs(z)))


def log_forget(s, w_f, b_f):
    m, d = s.shape
    n = w_f.shape[1]
    tm = _tile(m, 1024)
    return pl.pallas_call(
        _logf_kernel,
        grid=(m // tm,),
        in_specs=[pl.BlockSpec((tm, d), lambda i: (i, 0)),
                  pl.BlockSpec((d, n), lambda i: (0, 0)),
                  pl.BlockSpec((1, n), lambda i: (0, 0))],
        out_specs=pl.BlockSpec((tm, n), lambda i: (i, 0)),
        out_shape=jax.ShapeDtypeStruct((m, n), F32),
        compiler_params=_params("parallel"),
        name="log_forget",
    )(s, w_f, b_f)


def _split3(x):
    x1 = x.astype(BF16)
    r1 = x - x1.astype(F32)
    x2 = r1.astype(BF16)
    x3 = (r1 - x2.astype(F32)).astype(BF16)
    return x1, x2, x3


def _forget_bias_kernel(x_ref, pk_ref, pq_ref, kb_ref, qb_ref, carry_ref, *, ts, q_tile0):
    @pl.when(pl.program_id(1) == 0)
    def _():
        carry_ref[...] = jnp.zeros_like(carry_ref)

    row = lax.broadcasted_iota(jnp.int32, (ts, ts), 0)
    col = lax.broadcasted_iota(jnp.int32, (ts, ts), 1)
    tri = (col <= row).astype(BF16)
    y = carry_ref[...]
    for part in _split3(x_ref[...]):
        y = y + jnp.dot(tri, part, preferred_element_type=F32)
    carry_ref[...] = y[ts - 1:ts, :]

    z1, z2, z3 = _split3(y * (FOX_HEAD_DIM ** 0.5))
    lane = lax.broadcasted_iota(jnp.int32, z1.shape, 1)
    z1 = jnp.where(lane == LANES - 1, 1.0, z1).astype(BF16)
    zs = (z1, z2, z3)

    def placed(p_ref):
        return functools.reduce(jnp.add, [jnp.dot(z, p_ref[j], preferred_element_type=F32) for j, z in enumerate(zs)])

    kb_ref[...] = placed(pk_ref).astype(kb_ref.dtype)

    @pl.when(pl.program_id(1) >= q_tile0)
    def _():
        qb_ref[...] = placed(pq_ref).astype(qb_ref.dtype)


def _bias_placement(n_kv):
    n_q = n_kv * GROUP
    assert n_q < LANES and 4 * GROUP <= LANES
    pk = np.zeros((3, LANES, n_kv * LANES), np.float32)
    pq = np.zeros((3, LANES, n_q * LANES), np.float32)
    for h in range(n_kv):
        for g in range(GROUP):
            hq = h * GROUP + g
            for j in range(3):
                pk[j, hq, h * LANES + GROUP * j + g] = -1.0
                pq[j, hq, hq * LANES + GROUP * 3 + j] = 1.0
                pq[0, LANES - 1, hq * LANES + GROUP * j + g] = 1.0
        for j in range(3):
            pk[0, LANES - 1, h * LANES + GROUP * 3 + j] = 1.0
    return jnp.asarray(pk, BF16), jnp.asarray(pq, BF16)


def forget_bias(logf, n_kv, q_start):
    b, s, n = logf.shape
    unit = LANES if s <= 4 * LANES else 4 * LANES
    s_pad = -(-s // unit) * unit
    if s_pad != s:
        logf = jnp.pad(logf, ((0, 0), (0, s_pad - s), (0, 0)))
    ts = _tile(s_pad, 1024, LANES)
    pk, pq = _bias_placement(n_kv)
    nk_l, nq_l = pk.shape[2], pq.shape[2]
    q_tile0 = q_start // ts
    q_row0 = q_tile0 * ts
    kb, qb = pl.pallas_call(
        functools.partial(_forget_bias_kernel, ts=ts, q_tile0=q_tile0),
        grid=(b, s_pad // ts),
        in_specs=[pl.BlockSpec((None, ts, n), lambda bi, i: (bi, i, 0)),
                  pl.BlockSpec((3, n, nk_l), lambda bi, i: (0, 0, 0)),
                  pl.BlockSpec((3, n, nq_l), lambda bi, i: (0, 0, 0))],
        out_specs=[pl.BlockSpec((None, ts, nk_l), lambda bi, i: (bi, i, 0)),
                   pl.BlockSpec((None, ts, nq_l), lambda bi, i: (bi, jnp.maximum(i - q_tile0, 0), 0))],
        out_shape=[jax.ShapeDtypeStruct((b, s_pad, nk_l), BF16),
                   jax.ShapeDtypeStruct((b, s_pad - q_row0, nq_l), BF16)],
        scratch_shapes=[pltpu.VMEM((1, n), F32)],
        compiler_params=_params("parallel", "arbitrary"),
        name="forget_bias",
    )(logf, pk, pq)
    if s_pad != s:
        kb = kb[:, :s]
    if (q_start - q_row0, qb.shape[1]) != (0, s - q_start):
        qb = qb[:, q_start - q_row0:s - q_row0]
    return kb, qb


def _conv_kernel(*refs, tt, has_prev):
    if has_prev:
        hist_ref, prev_ref, cur_ref, w_ref, b_ref, g_ref, lb_ref, o_ref, xs_ref, cb_ref = refs
    else:
        hist_ref, cur_ref, w_ref, b_ref, g_ref, lb_ref, o_ref, xs_ref, cb_ref = refs
    i = pl.program_id(1)

    @pl.when(i == 0)
    def _():
        xs_ref[0:CONV_HALO, :] = hist_ref[...]

    if has_prev:
        @pl.when(i > 0)
        def _():
            xs_ref[0:CONV_HALO, :] = prev_ref[...]

    xs_ref[CONV_HALO:CONV_HALO + tt, :] = cur_ref[...]
    first = CONV_HALO - CONV_STATE
    rows = CONV_HALO + tt
    for c0 in range(0, xs_ref.shape[1], LANES):
        lanes = pl.ds(c0, LANES)
        window = xs_ref[:, lanes]
        acc = None
        for r in range(SUBLANES):
            offsets = [o for o in range(r, CONV_HALO + 1, SUBLANES) if first <= o < first + CONV_WIDTH]
            if not offsets:
                continue
            shifted = pltpu.roll(window, rows - r, axis=0) if r else window
            for o in offsets:
                k = o - first
                term = w_ref[k:k + 1, lanes] * shifted[o - r:o - r + tt]
                acc = term if acc is None else acc + term
        cb_ref[:, lanes] = acc + b_ref[:, lanes]
    c = cb_ref[...]
    mu = jnp.mean(c, axis=-1, keepdims=True)
    xc = c - mu
    var = jnp.mean(xc * xc, axis=-1, keepdims=True)
    y = xc * lax.rsqrt(var + EPS) * g_ref[...] + lb_ref[...]
    o_ref[...] = (y * (1.0 / (1.0 + jnp.exp(-y)))).astype(o_ref.dtype)


def conv_module(glu, hist, w, b, ln_g, ln_b, *, tt=128):
    bsz, t, c = glu.shape
    tt = _tile(t, tt, CONV_HALO) if t > tt else t
    nt = t // tt
    has_prev = nt > 1
    r = tt // CONV_HALO
    vec = pl.BlockSpec((1, c), lambda bi, i: (0, 0))
    in_specs = [pl.BlockSpec((None, CONV_HALO, c), lambda bi, i: (bi, 0, 0))]
    args = [hist]
    if has_prev:
        in_specs.append(pl.BlockSpec((None, CONV_HALO, c), lambda bi, i: (bi, jnp.maximum(i * r - 1, 0), 0)))
        args.append(glu)
    in_specs += [pl.BlockSpec((None, tt, c), lambda bi, i: (bi, i, 0)),
                 pl.BlockSpec((CONV_WIDTH, c), lambda bi, i: (0, 0)), vec, vec, vec]
    args += [glu, w, b.reshape(1, c), ln_g.reshape(1, c), ln_b.reshape(1, c)]
    return pl.pallas_call(
        functools.partial(_conv_kernel, tt=tt, has_prev=has_prev),
        grid=(bsz, nt),
        in_specs=in_specs,
        out_specs=pl.BlockSpec((None, tt, c), lambda bi, i: (bi, i, 0)),
        out_shape=jax.ShapeDtypeStruct((bsz, t, c), BF16),
        scratch_shapes=[pltpu.VMEM((CONV_HALO + tt, c), F32), pltpu.VMEM((tt, c), F32)],
        compiler_params=_params("parallel", "arbitrary"),
        name="conv_module",
    )(*args)


def _mem_attn_kernel(q_ref, k_ref, v_ref, o_ref, *, hd):
    scale = hd ** -0.5
    for h in range(MEM_HEADS):
        sl = slice(h * hd, (h + 1) * hd)
        s = lax.dot_general(q_ref[:, sl], k_ref[:, sl], (((1,), (1,)), ((), ())),
                            preferred_element_type=F32) * scale
        m = jnp.max(s, axis=-1, keepdims=True)
        p = jnp.exp(s - m)
        p = p / jnp.sum(p, axis=-1, keepdims=True)
        o_ref[:, sl] = jnp.dot(p.astype(BF16), v_ref[:, sl], preferred_element_type=F32).astype(o_ref.dtype)


def memory_attention(q_arr, q_col_block, mk, mv, *, tt=512):
    bsz, t, _ = q_arr.shape
    mtok, width = mk.shape[1:]
    tt = _tile(t, tt)
    return pl.pallas_call(
        functools.partial(_mem_attn_kernel, hd=width // MEM_HEADS),
        grid=(bsz, t // tt),
        in_specs=[pl.BlockSpec((None, tt, width), lambda bi, i: (bi, i, q_col_block)),
                  pl.BlockSpec((None, mtok, width), lambda bi, i: (bi, 0, 0)),
                  pl.BlockSpec((None, mtok, width), lambda bi, i: (bi, 0, 0))],
        out_specs=pl.BlockSpec((None, tt, width), lambda bi, i: (bi, i, 0)),
        out_shape=jax.ShapeDtypeStruct((bsz, t, width), BF16),
        compiler_params=_params("parallel", "parallel"),
        name="memory_attention",
    )(q_arr, mk, mv)


def _fox_kernel(q_ref, qb_ref, k_ref, kb_ref, v_ref, o_ref, m_ref, acc_ref, *, tq, tk, p_len, hps):
    qi = pl.program_id(2)
    ki = pl.program_id(3)
    dh = FOX_HEAD_DIM
    log2e_scale = (dh ** -0.5) * 1.4426950408889634

    @pl.when(ki == 0)
    def _():
        m_ref[...] = jnp.full_like(m_ref, NEG_INF)
        acc_ref[...] = jnp.zeros_like(acc_ref)

    q_first = p_len + qi * tq
    k_first = ki * tk
    needed = k_first <= q_first + tq - 1
    crosses_diagonal = k_first + tk - 1 > q_first

    def step(masked):
        if masked:
            q_pos = q_first + lax.broadcasted_iota(jnp.int32, (tq, tk), 0)
            k_pos = k_first + lax.broadcasted_iota(jnp.int32, (tq, tk), 1)
            visible = k_pos <= q_pos
        for hh in range(hps):
            ks = slice(hh * dh, (hh + 1) * dh)
            k_aug = jnp.concatenate([k_ref[:, ks], kb_ref[:, ks]], axis=1)
            v_aug = jnp.concatenate([v_ref[:, ks], jnp.ones((tk, dh), BF16)], axis=1)
            for g in range(GROUP):
                hq = hh * GROUP + g
                sl = slice(hq * dh, (hq + 1) * dh)
                q_aug = jnp.concatenate([q_ref[:, sl], qb_ref[:, sl]], axis=1)
                r = lax.dot_general(q_aug, k_aug, (((1,), (1,)), ((), ())), preferred_element_type=F32)
                if masked:
                    r = jnp.where(visible, r, NEG_INF)
                m_prev = m_ref[hq]
                m_new = jnp.maximum(m_prev, jnp.max(r, axis=-1, keepdims=True))
                alpha = jnp.exp2((m_prev - m_new) * log2e_scale)
                p = jnp.exp2((r - m_new) * log2e_scale)
                acc_ref[hq] = alpha * acc_ref[hq] + jnp.dot(p.astype(BF16), v_aug, preferred_element_type=F32)
                m_ref[hq] = m_new

    @pl.when(jnp.logical_and(needed, jnp.logical_not(crosses_diagonal)))
    def _():
        step(False)

    @pl.when(jnp.logical_and(needed, crosses_diagonal))
    def _():
        step(True)

    @pl.when(ki == pl.num_programs(3) - 1)
    def _():
        for hq in range(hps * GROUP):
            o_ref[:, hq * dh:(hq + 1) * dh] = (acc_ref[hq, :, 0:dh] / acc_ref[hq, :, dh:2 * dh]).astype(o_ref.dtype)


def forgetting_attention(q_arr, qb, k_arr, kb, v_arr, p_len, *, tq=512, tk=1024, heads_per_step=4):
    bsz, t, _ = q_arr.shape
    s_len = k_arr.shape[1]
    dh = FOX_HEAD_DIM
    n_kv = k_arr.shape[2] // dh
    tq, tk = _tile(t, tq), _tile(s_len, tk, LANES)
    hps = _tile(n_kv, heads_per_step, 1)
    gw = hps * GROUP * dh

    def kv_block(qi, ki):
        return jnp.minimum(ki, (p_len + (qi + 1) * tq - 1) // tk)

    q_spec = pl.BlockSpec((None, tq, gw), lambda b, h, qi, ki: (b, qi, h))
    kv_spec = pl.BlockSpec((None, tk, hps * dh), lambda b, h, qi, ki: (b, kv_block(qi, ki), h))
    return pl.pallas_call(
        functools.partial(_fox_kernel, tq=tq, tk=tk, p_len=p_len, hps=hps),
        grid=(bsz, n_kv // hps, t // tq, s_len // tk),
        in_specs=[q_spec, q_spec, kv_spec, kv_spec, kv_spec],
        out_specs=q_spec,
        out_shape=jax.ShapeDtypeStruct((bsz, t, n_kv * GROUP * dh), BF16),
        scratch_shapes=[pltpu.VMEM((hps * GROUP, tq, 1), F32), pltpu.VMEM((hps * GROUP, tq, 2 * dh), F32)],
        compiler_params=_params("parallel", "parallel", "parallel", "arbitrary"),
        name="forgetting_attention",
    )(q_arr, qb, k_arr, kb, v_arr)


def _weight_operands(w, mix_width, kv_width):
    depth, n_a = w["w_up"].shape[0], w["w_in_a"].shape[0]
    ops = {("kv_k", 0): (w["w_kv"], 0, 0), ("kv_v", 0): (w["w_kv"], 0, kv_width)}
    for layer in range(depth):
        if layer < n_a:
            ops["in_a", layer] = (w["w_in_a"], layer, 0)
            ops["in_g", layer] = (w["w_in_a"], layer, mix_width)
            ops["in_q", layer] = (w["w_in_a"], layer, 2 * mix_width)
        else:
            ops["in_b", layer] = (w["w_in_b"], layer - n_a, 0)
        ops["out_mix", layer] = (w["w_out"], layer, 0)
        ops["out_mem", layer] = (w["w_out"], layer, mix_width)
        ops["up", layer] = (w["w_up"], layer, 0)
        ops["down", layer] = (w["w_down"], layer, 0)
    return ops


def _run_group(x, conv_hist, mem_k, mem_v, past_k, past_v, past_logf, w, ops, emit):
    bsz, t, d = x.shape
    depth = w["norm_g"].shape[0]
    n_a = w["w_in_a"].shape[0]
    mem_width = mem_k.shape[-1]
    mix_width = d - mem_width
    n_q = w["b_f"].shape[0]
    n_kv = n_q // GROUP
    kv_width = n_kv * FOX_HEAD_DIM
    p_len = past_k.shape[1]
    m = bsz * t
    norm_g = w["norm_g"]

    def keep(copies, *keys):
        if emit:
            for key, wb in zip(keys, copies):
                ops[key] = (wb, 0, 0)

    xf = x.reshape(m, d)
    h = norm_cast(xf, norm_g[0, 0].reshape(1, 1, d))[0]
    conv_new = []
    kv_out = None
    for layer in range(depth):
        g = norm_g[layer]
        if layer < n_a:
            glu, *wb = matmul_glu(h, ops["in_a", layer], ops["in_g", layer], mix_width, emit=emit)
            keep(wb, ("in_a", layer), ("in_g", layer))
            glu = glu.reshape(bsz, t, mix_width)
            qm, *wb = matmul(h[None], ops["in_q", layer], ncols=mem_width, out_dtypes=(BF16,), emit=emit)
            keep(wb, ("in_q", layer))
            hist = conv_hist[layer]
            conv_new.append(glu[:, t - CONV_STATE:] if t >= CONV_STATE
                            else jnp.concatenate([hist[:, t:], glu], axis=1))
            hist_pad = jnp.pad(hist, ((0, 0), (CONV_HALO - CONV_STATE, 0), (0, 0)))
            mix = conv_module(glu, hist_pad, w["conv_w"][layer], w["conv_b"][layer],
                              w["conv_ln_g"][layer], w["conv_ln_b"][layer])
            mem_o = memory_attention(qm[0].reshape(bsz, t, mem_width), 0, mem_k[layer], mem_v[layer])
        else:
            u, *wb = matmul(h[None], ops["in_b", layer], ncols=d, out_dtypes=(BF16,), emit=emit)
            keep(wb, ("in_b", layer))
            u = u[0].reshape(bsz, t, d)
            qb, k_arr, kb, v_arr = kv_out
            mix = forgetting_attention(u, qb, k_arr, kb, v_arr, p_len, tk=2048 if layer == depth - 1 else 1024)
            mem_o = memory_attention(u, mix_width // mem_width, mem_k[layer], mem_v[layer])
        o, *wb = matmul_concat(mix.reshape(m, mix_width), mem_o.reshape(m, mem_width),
                               ops["out_mix", layer], ops["out_mem", layer], emit=emit)
        keep(wb, ("out_mix", layer), ("out_mem", layer))
        xf, (h2,) = residual_norm(xf, o, g[1], [g[2]])
        hid, *wb = matmul(h2[None], ops["up", layer], ncols=w["w_up"].shape[2], out_dtypes=(BF16,), act="relu2",
                          emit=emit)
        keep(wb, ("up", layer))
        if not emit and layer < n_a:
            f = matmul_row_panel(hid, ops["down", layer])
        else:
            f, *wb = matmul(hid, ops["down", layer], ncols=d, emit=emit)
            keep(wb, ("down", layer))
        next_gains = [norm_g[layer + 1, 0]] if layer + 1 < depth else []
        if layer == n_a - 1:
            next_gains.append(w["kv_norm_g"])
        xf, hs = residual_norm(xf, f[0], g[3], next_gains)
        if layer + 1 < depth:
            h = hs[0]
        if layer == n_a - 1:
            s = hs[-1]
            k_new, k_bf, *wb = matmul(s[None], ops["kv_k", 0], ncols=kv_width, out_dtypes=(F32, BF16), emit=emit)
            keep(wb, ("kv_k", 0))
            v_new, v_bf, *wb = matmul(s[None], ops["kv_v", 0], ncols=kv_width, out_dtypes=(F32, BF16), emit=emit)
            keep(wb, ("kv_v", 0))
            logf_pad = log_forget(s, w["w_f"], w["b_f_pad"]).reshape(bsz, t, LANES)
            logf_new = logf_pad[:, :, :n_q]
            if p_len:
                past_pad = jnp.pad(past_logf.astype(F32), ((0, 0), (0, 0), (0, LANES - n_q)))
                logf_all = jnp.concatenate([past_pad, logf_pad], axis=1)
                k_arr = jnp.concatenate([past_k.reshape(bsz, p_len, kv_width).astype(BF16),
                                         k_bf[0].reshape(bsz, t, kv_width)], axis=1)
                v_arr = jnp.concatenate([past_v.reshape(bsz, p_len, kv_width).astype(BF16),
                                         v_bf[0].reshape(bsz, t, kv_width)], axis=1)
            else:
                logf_all = logf_pad
                k_arr = k_bf[0].reshape(bsz, t, kv_width)
                v_arr = v_bf[0].reshape(bsz, t, kv_width)
            kb, qb = forget_bias(logf_all, n_kv, p_len)
            kv_out = (qb, k_arr, kb, v_arr)
            k_new = k_new[0].reshape(bsz, t, n_kv, FOX_HEAD_DIM)
            v_new = v_new[0].reshape(bsz, t, n_kv, FOX_HEAD_DIM)
    return xf.reshape(bsz, t, d), jnp.stack(conv_new), k_new, v_new, logf_new


def kernel(x_prompt, x_sample, cache_k, cache_v, cache_logf, cache_mem_k, cache_mem_v, state_conv, mem_prompt,
           norm_g, mem_norm_g, w_mem_kv, w_in_a, conv_w, conv_b, conv_ln_g, conv_ln_b, w_in_b, kv_norm_g, w_kv,
           w_f, b_f, w_out, w_up, w_down):
    depth, d = mem_norm_g.shape
    bsz, mtok, _ = mem_prompt.shape
    mem_width = w_mem_kv.shape[2] // 2
    hd = mem_width // MEM_HEADS
    n_q = b_f.shape[0]
    n_a = w_in_a.shape[0]
    n_kv = n_q // GROUP
    w = {
        "norm_g": norm_g, "kv_norm_g": kv_norm_g, "conv_w": conv_w, "conv_b": conv_b,
        "conv_ln_g": conv_ln_g, "conv_ln_b": conv_ln_b, "b_f": b_f,
        "w_in_a": w_in_a, "w_in_b": w_in_b, "w_kv": w_kv[None], "w_out": w_out, "w_up": w_up, "w_down": w_down,
        "w_f": jnp.pad(w_f, ((0, 0), (0, LANES - n_q))).astype(BF16),
        "b_f_pad": jnp.pad(b_f, (0, LANES - n_q)).reshape(1, LANES),
    }

    mem_h = norm_cast(mem_prompt.reshape(bsz * mtok, d), mem_norm_g.reshape(depth, 1, d))
    w_mem = w_mem_kv if bsz * mtok <= 1024 else w_mem_kv.astype(BF16)
    mk, mk_bf = matmul(mem_h, (w_mem, 0, 0), ncols=mem_width, out_dtypes=(F32, BF16))
    mv, mv_bf = matmul(mem_h, (w_mem, 0, mem_width), ncols=mem_width, out_dtypes=(F32, BF16))
    mem_k_prompt = mk.reshape(depth, bsz, mtok, MEM_HEADS, hd)
    mem_v_prompt = mv.reshape(depth, bsz, mtok, MEM_HEADS, hd)

    dbsz, dseq = x_sample.shape[:2]
    emit = dbsz * dseq <= 1024
    if not emit:
        for name in ("w_in_a", "w_in_b", "w_kv", "w_out", "w_up", "w_down"):
            w[name] = w[name].astype(BF16)
    ops = _weight_operands(w, d - mem_width, n_kv * FOX_HEAD_DIM)
    y_sample, conv_sample, k_sample, v_sample, logf_sample = _run_group(
        x_sample, state_conv, cache_mem_k.reshape(depth, dbsz, mtok, mem_width).astype(BF16),
        cache_mem_v.reshape(depth, dbsz, mtok, mem_width).astype(BF16), cache_k, cache_v, cache_logf, w, ops, emit)

    zero_conv = jnp.zeros((n_a, bsz, CONV_STATE, d - mem_width), x_prompt.dtype)
    empty_kv = jnp.zeros((bsz, 0, n_kv, FOX_HEAD_DIM), x_prompt.dtype)
    empty_f = jnp.zeros((bsz, 0, n_q), F32)
    y_prompt, conv_prompt, k_prompt, v_prompt, logf_prompt = _run_group(
        x_prompt, zero_conv, mk_bf.reshape(depth, bsz, mtok, mem_width), mv_bf.reshape(depth, bsz, mtok, mem_width),
        empty_kv, empty_kv, empty_f, w, ops, False)

    return (y_prompt, y_sample, k_prompt, v_prompt, logf_prompt, mem_k_prompt, mem_v_prompt, conv_prompt,
            k_sample, v_sample, logf_sample, conv_sample)
```
